```python
import math
import jax
import jax.numpy as jnp
from jax import lax

D_MODEL = 2048
BATCH = 16
SEQ = 256
DEPTH = 4
DEC_BATCH = 8
DEC_SEQ = 2048
PAST_LEN = 256

GRID_W = 64
D_MIX = 2 * D_MODEL
SSD_WIDTH = D_MIX // 2
SSD_HEAD_DIM = 64
SSD_HEADS = SSD_WIDTH // SSD_HEAD_DIM
SSD_GROUPS = 4
SSD_D_STATE = 128
SSD_CONV_K = 5
SSD_CHUNK = 128
SSD_CONV_CH = SSD_WIDTH + 2 * SSD_GROUPS * SSD_D_STATE
CONF_WIDTH = D_MIX // 4
CONF_CONV_K = 31
SC_WIDTH = D_MIX // 4
SC_CONV_K = 3
N_EXPERTS = 16
EXPERT_FF = D_MODEL
EC_CAPACITY_FACTOR = 2
N_MOD = 6
EPS = 1e-6
OFF_Z = SSD_WIDTH
OFF_XBC = OFF_Z + SSD_CONV_CH
OFF_DT = OFF_XBC + 2 * SSD_HEADS
OFF_CONF = OFF_DT + 2 * CONF_WIDTH
N_IN = OFF_CONF + 3 * SC_WIDTH

kernel_name = 'hybrid_ssd_conformer_shortconv_ec_dit_step'


def rmsnorm(x, g):
    xf = x.astype(jnp.float32)
    y = xf * lax.rsqrt(jnp.mean(xf * xf, axis=-1, keepdims=True) + EPS) * g.astype(jnp.float32)
    return y.astype(x.dtype)


def layernorm(x, g, b):
    xf = x.astype(jnp.float32)
    mu = jnp.mean(xf, axis=-1, keepdims=True)
    var = jnp.mean(jnp.square(xf - mu), axis=-1, keepdims=True)
    y = (xf - mu) * lax.rsqrt(var + EPS) * g.astype(jnp.float32) + b.astype(jnp.float32)
    return y.astype(x.dtype)


def dwconv(u, w, bias=None):
    k = w.shape[0]
    ch = u.shape[-1]
    y = lax.conv_general_dilated(u, w.astype(u.dtype)[:, None, :], (1,), [((k - 1) // 2, k // 2)],
                                 dimension_numbers=('NWC', 'WIO', 'NWC'), feature_group_count=ch)
    return y if bias is None else y + bias.astype(u.dtype)


def conv_grid_rows(u, w, bias, rows):
    b, l, ch = u.shape
    return dwconv(u.reshape(b * rows, GRID_W, ch), w, bias).reshape(b, l, ch)


def conv_grid_cols(u, w, rows):
    b, l, ch = u.shape
    t = u.reshape(b, rows, GRID_W, ch).transpose(0, 2, 1, 3).reshape(b * GRID_W, rows, ch)
    t = dwconv(t, w)
    return t.reshape(b, GRID_W, rows, ch).transpose(0, 2, 1, 3).reshape(b, l, ch)


def ssd_scan(x, dt, a, bm, cm, h0):
    b, l, nh, p = x.shape
    g, n = bm.shape[-2:]
    k = nh // g
    q = SSD_CHUNK
    nc = l // q
    x = x.reshape(b, nc, q, g, k, p)
    dt = dt.reshape(b, nc, q, g, k)
    bm = bm.reshape(b, nc, q, g, n)
    cm = cm.reshape(b, nc, q, g, n)
    acum = jnp.cumsum(dt * a.reshape(g, k), axis=2)
    seg = acum[:, :, :, None] - acum[:, :, None]
    lower = jnp.tril(jnp.ones((q, q), dtype=bool))[:, :, None, None]
    lmat = jnp.exp(jnp.where(lower, seg, -jnp.inf))
    cb = jnp.einsum('bcqgn,bcsgn->bcqsg', cm, bm)
    wmat = cb[..., None] * lmat * dt[:, :, None]
    y_diag = jnp.einsum('bcqsgk,bcsgkp->bcqgkp', wmat, x)
    decay_to_end = jnp.exp(acum[:, :, -1:] - acum)
    states = jnp.einsum('bcsgn,bcsgkp->bcgkpn', bm, x * (decay_to_end * dt)[..., None])
    chunk_decay = jnp.exp(acum[:, :, -1])

    def step(h, inp):
        st, dec = inp
        return dec[..., None, None] * h + st, h

    h_final, h_prev = lax.scan(step, h0.reshape(b, g, k, p, n),
                               (jnp.moveaxis(states, 1, 0), jnp.moveaxis(chunk_decay, 1, 0)))
    h_prev = jnp.moveaxis(h_prev, 0, 1)
    y_off = jnp.einsum('bcqgn,bcgkpn->bcqgkp', cm, h_prev) * jnp.exp(acum)[..., None]
    return (y_diag + y_off).reshape(b, l, nh, p), h_final.reshape(b, nh, p, n)


def mixer(h, lp, rows, h0f, h0b):
    b, l, _ = h.shape
    f32 = jnp.float32
    proj = jnp.einsum('bld,de->ble', h, lp['w_in'])
    z, xbc, dt_raw, conf_in, sc_in = jnp.split(proj, [OFF_Z, OFF_XBC, OFF_DT, OFF_CONF], axis=-1)
    xbc = jax.nn.silu(dwconv(xbc, lp['ssd_conv_w'], lp['ssd_conv_b']))
    xs, bm, cm = jnp.split(xbc, [SSD_WIDTH, SSD_WIDTH + SSD_GROUPS * SSD_D_STATE], axis=-1)
    xs = xs.astype(f32).reshape(b, l, SSD_HEADS, SSD_HEAD_DIM)
    bm = bm.astype(f32).reshape(b, l, SSD_GROUPS, SSD_D_STATE)
    cm = cm.astype(f32).reshape(b, l, SSD_GROUPS, SSD_D_STATE)
    dt = jax.nn.softplus(dt_raw.astype(f32).reshape(b, l, 2, SSD_HEADS) + lp['ssd_dt_bias'].astype(f32))
    a = -jnp.exp(lp['ssd_a_log'].astype(f32))
    y_f, s_f = ssd_scan(xs, dt[:, :, 0], a[0], bm, cm, h0f.astype(f32))
    flip = lambda t: jnp.flip(t, axis=1)
    y_b, s_b = ssd_scan(flip(xs), flip(dt[:, :, 1]), a[1], flip(bm), flip(cm), h0b.astype(f32))
    y = y_f + flip(y_b) + lp['ssd_d'].astype(f32)[:, None] * xs
    y = rmsnorm(y.reshape(b, l, SSD_WIDTH) * jax.nn.silu(z.astype(f32)), lp['ssd_norm_g']).astype(h.dtype)
    val, gt = jnp.split(conf_in, 2, axis=-1)
    u = val * jax.nn.sigmoid(gt)
    if rows is None:
        u = dwconv(u, lp['conf_conv_w'], lp['conf_conv_b'])
    else:
        u = conv_grid_rows(u, lp['conf_conv_w'], lp['conf_conv_b'], rows)
    u = jax.nn.silu(layernorm(u, lp['conf_ln_g'], lp['conf_ln_b']))
    bg, cg, hx = jnp.split(sc_in, 3, axis=-1)
    v = cg * hx
    v = dwconv(v, lp['sc_conv_w']) if rows is None else conv_grid_cols(v, lp['sc_conv_w'], rows)
    v = bg * v
    out = jnp.einsum('ble,ed->bld', jnp.concatenate([y, u, v], axis=-1), lp['w_out'])
    return out, s_f, s_b


def ec_moe(h, router_w, w_gate, w_up, w_down):
    b, t, d = h.shape
    cap = EC_CAPACITY_FACTOR * t // N_EXPERTS
    aff = jax.nn.softmax(jnp.einsum('btd,de->bte', h, router_w).astype(jnp.float32), axis=-1)
    gate_vals, tok_idx = lax.top_k(jnp.swapaxes(aff, 1, 2), cap)
    xe = jax.vmap(lambda hb, ib: hb[ib])(h, tok_idx)
    hid = jax.nn.silu(jnp.einsum('becd,edf->becf', xe, w_gate)) * jnp.einsum('becd,edf->becf', xe, w_up)
    ye = jnp.einsum('becf,efd->becd', hid, w_down) * gate_vals[..., None].astype(h.dtype)
    return jax.vmap(lambda yb, ib: jnp.zeros((t, d), h.dtype).at[ib.reshape(-1)].add(yb.reshape(-1, d)))(ye, tok_idx)


def adaln(cond, w, b):
    return (jnp.einsum('nd,de->ne', jax.nn.silu(cond), w) + b).reshape(-1, N_MOD, D_MODEL)


def block(x, mod, lp, rows, h0f, h0b):
    mod = mod.astype(x.dtype)
    sh1, sc1, g1, sh2, sc2, g2 = [mod[:, i][:, None, :] for i in range(N_MOD)]
    hn = rmsnorm(x, lp['g_mix']) * (1 + sc1) + sh1
    m, s_f, s_b = mixer(hn, lp, rows, h0f, h0b)
    x = x + g1 * m
    hn = rmsnorm(x, lp['g_ffn']) * (1 + sc2) + sh2
    x = x + g2 * ec_moe(hn, lp['router_w'], lp['w_gate'], lp['w_up'], lp['w_down'])
    return x, s_f, s_b


def setup_inputs(seed: int = 0) -> dict:
    key = jax.random.key(seed)
    ks = jax.random.split(key, 32)
    f32 = jnp.float32
    nrm = lambda k, shape, scale: jax.random.normal(k, shape, f32) * scale
    H = SSD_HEADS
    dt0 = jnp.exp(jax.random.uniform(ks[10], (DEPTH, 2, H), f32, math.log(1e-3), math.log(1e-1)))
    st_shape = (DEC_BATCH, DEPTH, SSD_HEADS, SSD_HEAD_DIM, SSD_D_STATE)
    return {
        'x_prompt': nrm(ks[0], (BATCH, SEQ, D_MODEL), 1.0),
        'x_sample': nrm(ks[1], (DEC_BATCH, DEC_SEQ, D_MODEL), 1.0),
        'state_ssd_fwd': nrm(ks[2], st_shape, 0.5),
        'state_ssd_bwd': nrm(ks[3], st_shape, 0.5),
        'c': nrm(ks[4], (DEC_BATCH, D_MODEL), 1.0),
        'c_ctx': nrm(ks[5], (D_MODEL,), 1.0),
        'w_ada': nrm(ks[6], (DEPTH, D_MODEL, N_MOD * D_MODEL), 0.5 * D_MODEL ** -0.5),
        'b_ada': nrm(ks[7], (DEPTH, N_MOD * D_MODEL), 0.02),
        'norm_mix_g': 1.0 + nrm(ks[8], (DEPTH, D_MODEL), 0.05),
        'norm_ffn_g': 1.0 + nrm(ks[9], (DEPTH, D_MODEL), 0.05),
        'w_in': nrm(ks[11], (DEPTH, D_MODEL, N_IN), D_MODEL ** -0.5),
        'ssd_conv_w': nrm(ks[12], (DEPTH, SSD_CONV_K, SSD_CONV_CH), SSD_CONV_K ** -0.5),
        'ssd_conv_b': nrm(ks[13], (DEPTH, SSD_CONV_CH), 0.02),
        'ssd_dt_bias': dt0 + jnp.log(-jnp.expm1(-dt0)),
        'ssd_a_log': jnp.log(jax.random.uniform(ks[14], (DEPTH, 2, H), f32, 1.0, 16.0)),
        'ssd_d': 1.0 + nrm(ks[15], (DEPTH, H), 0.1),
        'ssd_norm_g': 1.0 + nrm(ks[16], (DEPTH, SSD_WIDTH), 0.05),
        'conf_conv_w': nrm(ks[17], (DEPTH, CONF_CONV_K, CONF_WIDTH), CONF_CONV_K ** -0.5),
        'conf_conv_b': nrm(ks[18], (DEPTH, CONF_WIDTH), 0.02),
        'conf_ln_g': 1.0 + nrm(ks[19], (DEPTH, CONF_WIDTH), 0.05),
        'conf_ln_b': nrm(ks[20], (DEPTH, CONF_WIDTH), 0.02),
        'sc_conv_w': nrm(ks[21], (DEPTH, SC_CONV_K, SC_WIDTH), SC_CONV_K ** -0.5),
        'w_out': nrm(ks[22], (DEPTH, D_MIX, D_MODEL), D_MIX ** -0.5),
        'router_w': nrm(ks[23], (DEPTH, D_MODEL, N_EXPERTS), D_MODEL ** -0.5),
        'w_gate': nrm(ks[24], (DEPTH, N_EXPERTS, D_MODEL, EXPERT_FF), D_MODEL ** -0.5),
        'w_up': nrm(ks[25], (DEPTH, N_EXPERTS, D_MODEL, EXPERT_FF), D_MODEL ** -0.5),
        'w_down': nrm(ks[26], (DEPTH, N_EXPERTS, EXPERT_FF, D_MODEL), EXPERT_FF ** -0.5),
        'norm_final_g': 1.0 + nrm(ks[27], (D_MODEL,), 0.05),
    }


def reference(x_prompt, x_sample, state_ssd_fwd, state_ssd_bwd, c, c_ctx, w_ada, b_ada, norm_mix_g, norm_ffn_g,
              w_in, ssd_conv_w, ssd_conv_b, ssd_dt_bias, ssd_a_log, ssd_d, ssd_norm_g, conf_conv_w, conf_conv_b,
              conf_ln_g, conf_ln_b, sc_conv_w, w_out, router_w, w_gate, w_up, w_down, norm_final_g):
    rows = x_sample.shape[1] // GRID_W
    nb = x_prompt.shape[0]
    zero_state = jnp.zeros((nb, SSD_HEADS, SSD_HEAD_DIM, SSD_D_STATE), jnp.float32)
    xp, xs = x_prompt, x_sample
    new_f, new_b = [], []
    for l in range(DEPTH):
        lp = {
            'g_mix': norm_mix_g[l], 'g_ffn': norm_ffn_g[l], 'w_in': w_in[l],
            'ssd_conv_w': ssd_conv_w[l], 'ssd_conv_b': ssd_conv_b[l], 'ssd_dt_bias': ssd_dt_bias[l],
            'ssd_a_log': ssd_a_log[l], 'ssd_d': ssd_d[l], 'ssd_norm_g': ssd_norm_g[l],
            'conf_conv_w': conf_conv_w[l], 'conf_conv_b': conf_conv_b[l],
            'conf_ln_g': conf_ln_g[l], 'conf_ln_b': conf_ln_b[l], 'sc_conv_w': sc_conv_w[l],
            'w_out': w_out[l], 'router_w': router_w[l], 'w_gate': w_gate[l], 'w_up': w_up[l], 'w_down': w_down[l],
        }
        mod_ctx = adaln(c_ctx[None, :], w_ada[l], b_ada[l])
        mod_lat = adaln(c, w_ada[l], b_ada[l])
        xp, s_f, s_b = block(xp, mod_ctx, lp, None, zero_state, zero_state)
        new_f.append(s_f)
        new_b.append(s_b)
        xs, _, _ = block(xs, mod_lat, lp, rows, state_ssd_fwd[:, l], state_ssd_bwd[:, l])
    y_prompt = rmsnorm(xp, norm_final_g)
    y_sample = rmsnorm(xs, norm_final_g)
    new_state_ssd_fwd = jnp.stack(new_f, axis=1).astype(x_prompt.dtype)
    new_state_ssd_bwd = jnp.stack(new_b, axis=1).astype(x_prompt.dtype)
    return (y_prompt, y_sample, new_state_ssd_fwd, new_state_ssd_bwd)
```

```python
import functools

import jax
import jax.numpy as jnp
from jax import lax
from jax.experimental import pallas as pl
from jax.experimental.pallas import tpu as pltpu

F32 = jnp.float32
BF16 = jnp.bfloat16
I32 = jnp.int32
EPS = 1e-6
HIGHEST = lax.Precision.HIGHEST

GRID_W = 64
SSD_CHUNK = 128
SSD_HEAD_DIM = 64
SSD_GROUPS = 4
SSD_D_STATE = 128
N_MOD = 6
EC_CAPACITY_FACTOR = 2
LANES = 128
MIB = 1024 * 1024
NEG = -1e30


def _cparams(n_grid, vmem_mib):
    return pltpu.CompilerParams(dimension_semantics=("arbitrary",) * n_grid, vmem_limit_bytes=vmem_mib * MIB)


def _silu(x):
    return x * jax.nn.sigmoid(x)


def _softplus(x):
    return jnp.maximum(x, 0.0) + jnp.log1p(jnp.exp(-jnp.abs(x)))


def _mod_row(i, tm, n_ctx_rows, lat_len):
    nct = n_ctx_rows // tm
    per = lat_len // tm
    return jnp.where(i < nct, 0, 1 + (i - nct) // per)


def _adaln_body(cond_ref, w_ref, b_ref, o_ref):
    s = _silu(cond_ref[...]).astype(BF16)
    o_ref[0] = jnp.dot(s, w_ref[0].astype(BF16), preferred_element_type=F32) + b_ref[0]


def _adaln_all(cond, w_ada, b_ada):
    depth, d, n = w_ada.shape
    rows = cond.shape[0]
    tn = 1024
    return pl.pallas_call(
        _adaln_body,
        grid=(depth, n // tn),
        in_specs=[pl.BlockSpec((rows, d), lambda l, j: (0, 0)),
                  pl.BlockSpec((1, d, tn), lambda l, j: (l, 0, j)),
                  pl.BlockSpec((1, 1, tn), lambda l, j: (l, 0, j))],
        out_specs=pl.BlockSpec((1, rows, tn), lambda l, j: (l, 0, j)),
        out_shape=jax.ShapeDtypeStruct((depth, rows, n), F32),
        compiler_params=_cparams(2, 40),
        name="adaln",
    )(cond, w_ada, b_ada.reshape(depth, 1, n))


def _modnorm(x, g, sc, sh):
    r = lax.rsqrt(jnp.mean(x * x, axis=-1, keepdims=True) + EPS)
    return (x * r * g) * (1.0 + sc) + sh


def _inproj_body(x_ref, g_ref, sc_ref, sh_ref, w_ref, wdt_ref, o_ref, dt_ref, hn_ref):
    @pl.when(pl.program_id(1) == 0)
    def _():
        hb = _modnorm(x_ref[...], g_ref[...], sc_ref[0], sh_ref[0]).astype(BF16)
        hn_ref[...] = hb
        dt_ref[...] = jnp.dot(hb, wdt_ref[...], preferred_element_type=F32)

    o_ref[...] = jnp.dot(hn_ref[...], w_ref[...], preferred_element_type=F32)


def _inproj(x, g, mod, w_main, w_dt, n_ctx_rows, lat_len):
    m, d = x.shape
    n = w_main.shape[1]
    ndt = w_dt.shape[1]
    tm, tn = 512, 1024
    mrow = functools.partial(_mod_row, tm=tm, n_ctx_rows=n_ctx_rows, lat_len=lat_len)
    return pl.pallas_call(
        _inproj_body,
        grid=(m // tm, n // tn),
        in_specs=[pl.BlockSpec((tm, d), lambda i, j: (i, 0)),
                  pl.BlockSpec((1, d), lambda i, j: (0, 0)),
                  pl.BlockSpec((1, 1, d), lambda i, j: (mrow(i) * N_MOD + 1, 0, 0)),
                  pl.BlockSpec((1, 1, d), lambda i, j: (mrow(i) * N_MOD + 0, 0, 0)),
                  pl.BlockSpec((d, tn), lambda i, j: (0, j)),
                  pl.BlockSpec((d, ndt), lambda i, j: (0, 0))],
        out_specs=[pl.BlockSpec((tm, tn), lambda i, j: (i, j)),
                   pl.BlockSpec((tm, ndt), lambda i, j: (i, 0))],
        out_shape=[jax.ShapeDtypeStruct((m, n), F32), jax.ShapeDtypeStruct((m, ndt), F32)],
        scratch_shapes=[pltpu.VMEM((tm, d), BF16)],
        compiler_params=_cparams(2, 40),
        name="inproj",
    )(x, g.reshape(1, d), mod, mod, w_main, w_dt)


def _shift_rows(x, d, pos, seg):
    if d == 0:
        return x
    n = x.shape[0]
    y = pltpu.roll(x, (-d) % n, axis=0)
    ok = (pos >= -d) if d < 0 else (pos < seg - d)
    return jnp.where(ok, y, 0.0)


def _ssdconv_body(nct, seg_c, seg_l, x_ref, w_ref, b_ref, o_ref):
    seg = jnp.where(pl.program_id(0) < nct, seg_c, seg_l)
    x = x_ref[...]
    k = w_ref.shape[0]
    pos = lax.broadcasted_iota(I32, x.shape, 0) & (seg - 1)
    acc = jnp.broadcast_to(b_ref[...], x.shape)
    for j in range(k):
        acc = acc + _shift_rows(x, j - (k - 1) // 2, pos, seg) * w_ref[j:j + 1, :]
    o_ref[...] = _silu(acc)


def _ssdconv(proj, col0, w, b, n_ctx_rows, ctx_len, lat_len):
    m = proj.shape[0]
    k, ch = w.shape
    tr, tc = lat_len, 256
    cb0 = col0 // tc
    return pl.pallas_call(
        functools.partial(_ssdconv_body, n_ctx_rows // tr, ctx_len, lat_len),
        grid=(m // tr, ch // tc),
        in_specs=[pl.BlockSpec((tr, tc), lambda i, j: (i, cb0 + j)),
                  pl.BlockSpec((k, tc), lambda i, j: (0, j)),
                  pl.BlockSpec((1, tc), lambda i, j: (0, j))],
        out_specs=pl.BlockSpec((tr, tc), lambda i, j: (i, j)),
        out_shape=jax.ShapeDtypeStruct((m, ch), F32),
        compiler_params=_cparams(2, 40),
        name="ssdconv",
    )(proj, w, b.reshape(1, ch))


def _ssd_body(nc, has_h0, want_final, *refs):
    refs = list(refs)
    xs_ref, bm_ref, cm_ref, dt_ref, bias_ref, alog_ref = refs[:6]
    refs = refs[6:]
    if has_h0:
        h0f_ref, h0b_ref = refs[:2]
        refs = refs[3:]
    y_ref = refs[0]
    refs = refs[1:]
    if want_final:
        fin_ref = refs[0]
        refs = refs[1:]
    st_ref = refs[0]

    d = pl.program_id(1)
    c = pl.program_id(2)
    q = SSD_CHUNK
    p = SSD_HEAD_DIM
    nst = SSD_D_STATE
    heads_per_group = xs_ref.shape[1] // p // SSD_GROUPS

    @pl.when(c == 0)
    def _():
        if has_h0:
            @pl.when(d == 0)
            def _():
                st_ref[...] = h0f_ref[0, 0].T

            @pl.when(d == 1)
            def _():
                st_ref[...] = h0b_ref[0, 0].T
        else:
            st_ref[...] = jnp.zeros(st_ref.shape, F32)

    x = xs_ref[...]
    xb = x.astype(BF16)
    bmat = bm_ref[...]
    cmat = cm_ref[...]
    dt = _softplus(dt_ref[...] + bias_ref[0])
    a = -jnp.exp(alog_ref[0])
    dta = dt * a
    row = lax.broadcasted_iota(I32, (q, q), 0)
    col = lax.broadcasted_iota(I32, (q, q), 1)
    msk = (row - col) * (1 - 2 * d) >= 0
    acum = jnp.dot(jnp.where(msk, 1.0, 0.0), dta, precision=HIGHEST, preferred_element_type=F32)
    total = jnp.sum(dta, axis=0, keepdims=True)
    ea = jnp.exp(acum)
    wst = jnp.exp(total - acum) * dt
    cd = jnp.exp(total)
    acum_t = acum.T
    dt_t = dt.T

    for g in range(SSD_GROUPS):
        cg = cmat[:, g * nst:(g + 1) * nst]
        bg = bmat[:, g * nst:(g + 1) * nst]
        bg_t = bg.T.astype(BF16)
        cb = lax.dot_general(cg.astype(BF16), bg.astype(BF16), (((1,), (1,)), ((), ())),
                             preferred_element_type=F32)
        for kk in range(heads_per_group):
            h = g * heads_per_group + kk
            sl = slice(h * p, (h + 1) * p)
            seg = acum[:, h:h + 1] - acum_t[h:h + 1, :]
            lm = jnp.exp(jnp.where(msk, seg, NEG))
            w = (cb * lm * dt_t[h:h + 1, :]).astype(BF16)
            ce = (cg * ea[:, h:h + 1]).astype(BF16)
            sth = st_ref[:, sl]
            lhs = jnp.concatenate([w, ce], axis=1)
            rhs = jnp.concatenate([xb[:, sl], sth.astype(BF16)], axis=0)
            y_ref[0, :, sl] = jnp.dot(lhs, rhs, preferred_element_type=F32)
            xdt = (x[:, sl] * wst[:, h:h + 1]).astype(BF16)
            st_ref[:, sl] = sth * cd[:, h:h + 1] + jnp.dot(bg_t, xdt, preferred_element_type=F32)

    if want_final:
        @pl.when(c == nc - 1)
        def _():
            fin_ref[0, 0] = st_ref[...].T


def _ssd_scan(xbc, dtraw, bias, alog, y_prev, h0, layer, n_seq, seq_len, row0, want_final):
    m = xbc.shape[0]
    q = SSD_CHUNK
    nc = seq_len // q
    gn = SSD_GROUPS * SSD_D_STATE
    hp = xbc.shape[1] - 2 * gn
    blk0 = row0 // q

    def rb(s, d, c):
        return blk0 + s * nc + c + d * (nc - 1 - 2 * c)

    in_specs = [pl.BlockSpec((q, hp), lambda s, d, c: (rb(s, d, c), 0)),
                pl.BlockSpec((q, gn), lambda s, d, c: (rb(s, d, c), hp // gn)),
                pl.BlockSpec((q, gn), lambda s, d, c: (rb(s, d, c), hp // gn + 1)),
                pl.BlockSpec((q, LANES), lambda s, d, c: (rb(s, d, c), d)),
                pl.BlockSpec((1, 1, LANES), lambda s, d, c: (d, 0, 0)),
                pl.BlockSpec((1, 1, LANES), lambda s, d, c: (d, 0, 0))]
    args = [xbc, xbc, xbc, dtraw, bias, alog]
    aliases = {}
    if h0 is not None:
        in_specs += [pl.BlockSpec((1, 1, hp, SSD_D_STATE), lambda s, d, c: (s, layer, 0, 0)),
                     pl.BlockSpec((1, 1, hp, SSD_D_STATE), lambda s, d, c: (s, layer, 0, 0))]
        args += [h0[0], h0[1]]
    if y_prev is not None:
        in_specs.append(pl.BlockSpec(memory_space=pl.ANY))
        args.append(y_prev)
        aliases = {len(args) - 1: 0}
    out_specs = [pl.BlockSpec((1, q, hp), lambda s, d, c: (d, rb(s, d, c), 0))]
    out_shape = [jax.ShapeDtypeStruct((2, m, hp), F32)]
    if want_final:
        out_specs.append(pl.BlockSpec((1, 1, hp, SSD_D_STATE), lambda s, d, c: (d, s, 0, 0)))
        out_shape.append(jax.ShapeDtypeStruct((2, n_seq, hp, SSD_D_STATE), F32))
    has_h0 = h0 is not None
    assert has_h0 == (y_prev is not None)
    return pl.pallas_call(
        functools.partial(_ssd_body, nc, has_h0, want_final),
        grid=(n_seq, 2, nc),
        in_specs=in_specs,
        out_specs=out_specs,
        out_shape=out_shape,
        scratch_shapes=[pltpu.VMEM((SSD_D_STATE, hp), F32)],
        input_output_aliases=aliases,
        compiler_params=_cparams(3, 40),
        name="ssd_scan",
    )(*args)


def _gate_body(yf_ref, yb_ref, xs_ref, z_ref, dexp_ref, g_ref, o_ref):
    y = yf_ref[0] + yb_ref[0] + dexp_ref[...] * xs_ref[...]
    v = y * _silu(z_ref[...])
    r = lax.rsqrt(jnp.mean(v * v, axis=-1, keepdims=True) + EPS)
    o_ref[...] = (v * r * g_ref[...]).astype(BF16)


def _ssd_gate(y2, xbc, proj, dexp, g):
    m = proj.shape[0]
    hp = y2.shape[2]
    tm = 256
    return pl.pallas_call(
        _gate_body,
        grid=(m // tm,),
        in_specs=[pl.BlockSpec((1, tm, hp), lambda i: (0, i, 0)),
                  pl.BlockSpec((1, tm, hp), lambda i: (1, i, 0)),
                  pl.BlockSpec((tm, hp), lambda i: (i, 0)),
                  pl.BlockSpec((tm, hp), lambda i: (i, 0)),
                  pl.BlockSpec((1, hp), lambda i: (0, 0)),
                  pl.BlockSpec((1, hp), lambda i: (0, 0))],
        out_specs=pl.BlockSpec((tm, hp), lambda i: (i, 0)),
        out_shape=jax.ShapeDtypeStruct((m, hp), BF16),
        compiler_params=_cparams(1, 40),
        name="ssd_gate",
    )(y2, y2, xbc, proj, dexp.reshape(1, hp), g.reshape(1, hp))


def _conf_body(nct, seg_c, seg_l, val_ref, gt_ref, w_ref, b_ref, lg_ref, lb_ref, o_ref):
    seg = jnp.where(pl.program_id(0) < nct, seg_c, seg_l)
    u = val_ref[...] * jax.nn.sigmoid(gt_ref[...])
    k = w_ref.shape[0]
    pos = lax.broadcasted_iota(I32, u.shape, 0) & (seg - 1)
    acc = jnp.broadcast_to(b_ref[...], u.shape)
    for j in range(k):
        acc = acc + _shift_rows(u, j - (k - 1) // 2, pos, seg) * w_ref[j:j + 1, :]
    mu = jnp.mean(acc, axis=-1, keepdims=True)
    cen = acc - mu
    var = jnp.mean(cen * cen, axis=-1, keepdims=True)
    y = cen * lax.rsqrt(var + EPS) * lg_ref[...] + lb_ref[...]
    o_ref[...] = _silu(y).astype(BF16)


def _conformer(proj, col0, w, b, lg, lb, n_ctx_rows, ctx_len):
    m = proj.shape[0]
    k, ch = w.shape
    tm = ctx_len
    cb0 = col0 // ch
    vec = lambda v: v.reshape(1, ch)
    vspec = pl.BlockSpec((1, ch), lambda i: (0, 0))
    return pl.pallas_call(
        functools.partial(_conf_body, n_ctx_rows // tm, ctx_len, GRID_W),
        grid=(m // tm,),
        in_specs=[pl.BlockSpec((tm, ch), lambda i: (i, cb0)),
                  pl.BlockSpec((tm, ch), lambda i: (i, cb0 + 1)),
                  pl.BlockSpec((k, ch), lambda i: (0, 0)),
                  vspec, vspec, vspec],
        out_specs=pl.BlockSpec((tm, ch), lambda i: (i, 0)),
        out_shape=jax.ShapeDtypeStruct((m, ch), BF16),
        compiler_params=_cparams(1, 40),
        name="conformer",
    )(proj, proj, w, vec(b), vec(lg), vec(lb))


def _sconv_body(nct, seg_c, bg_ref, cg_ref, hx_ref, w_ref, o_ref):
    i = pl.program_id(0)
    v = cg_ref[...] * hx_ref[...]
    n = v.shape[0]
    it = lax.broadcasted_iota(I32, v.shape, 0)
    w0, w1, w2 = w_ref[0:1, :], w_ref[1:2, :], w_ref[2:3, :]

    @pl.when(i < nct)
    def _():
        pos = it & (seg_c - 1)
        y = w1 * v + w0 * _shift_rows(v, -1, pos, seg_c) + w2 * _shift_rows(v, 1, pos, seg_c)
        o_ref[...] = (bg_ref[...] * y).astype(BF16)

    @pl.when(i >= nct)
    def _():
        y = w1 * v + w0 * _shift_rows(v, -GRID_W, it, n) + w2 * _shift_rows(v, GRID_W, it, n)
        o_ref[...] = (bg_ref[...] * y).astype(BF16)


def _sconv(proj, col0, w, n_ctx_rows, ctx_len, lat_len):
    m = proj.shape[0]
    k, ch = w.shape
    tr, tc = lat_len, 256
    cb0 = col0 // tc
    nch = ch // tc
    return pl.pallas_call(
        functools.partial(_sconv_body, n_ctx_rows // tr, ctx_len),
        grid=(m // tr, nch),
        in_specs=[pl.BlockSpec((tr, tc), lambda i, j: (i, cb0 + j)),
                  pl.BlockSpec((tr, tc), lambda i, j: (i, cb0 + nch + j)),
                  pl.BlockSpec((tr, tc), lambda i, j: (i, cb0 + 2 * nch + j)),
                  pl.BlockSpec((k, tc), lambda i, j: (0, j))],
        out_specs=pl.BlockSpec((tr, tc), lambda i, j: (i, j)),
        out_shape=jax.ShapeDtypeStruct((m, ch), BF16),
        compiler_params=_cparams(2, 40),
        name="sconv",
    )(proj, proj, proj, w)


def _outproj_body(y_ref, u_ref, v_ref, w_ref, x_ref, g_ref, o_ref):
    ky, ku = y_ref.shape[1], u_ref.shape[1]
    acc = jnp.dot(y_ref[...], w_ref[0:ky, :], preferred_element_type=F32)
    acc = acc + jnp.dot(u_ref[...], w_ref[ky:ky + ku, :], preferred_element_type=F32)
    acc = acc + jnp.dot(v_ref[...], w_ref[ky + ku:, :], preferred_element_type=F32)
    o_ref[...] = x_ref[...] + g_ref[0] * acc


def _outproj(y, u, v, w, x, mod, n_ctx_rows, lat_len):
    m, d = x.shape
    kt = w.shape[0]
    tm, tn = 1024, 512
    mrow = functools.partial(_mod_row, tm=tm, n_ctx_rows=n_ctx_rows, lat_len=lat_len)
    return pl.pallas_call(
        _outproj_body,
        grid=(m // tm, d // tn),
        in_specs=[pl.BlockSpec((tm, y.shape[1]), lambda i, j: (i, 0)),
                  pl.BlockSpec((tm, u.shape[1]), lambda i, j: (i, 0)),
                  pl.BlockSpec((tm, v.shape[1]), lambda i, j: (i, 0)),
                  pl.BlockSpec((kt, tn), lambda i, j: (0, j)),
                  pl.BlockSpec((tm, tn), lambda i, j: (i, j)),
                  pl.BlockSpec((1, 1, tn), lambda i, j: (mrow(i) * N_MOD + 2, 0, j))],
        out_specs=pl.BlockSpec((tm, tn), lambda i, j: (i, j)),
        out_shape=jax.ShapeDtypeStruct((m, d), F32),
        compiler_params=_cparams(2, 48),
        name="outproj",
    )(y, u, v, w, x, mod)


def _router_body(n_exp, x_ref, g_ref, sc_ref, sh_ref, rw_ref, hn_ref, aff_t_ref):
    hn = _modnorm(x_ref[...], g_ref[...], sc_ref[0], sh_ref[0])
    hn_ref[...] = hn.astype(BF16)
    logits = jnp.dot(hn, rw_ref[...], precision=HIGHEST, preferred_element_type=F32)
    lane = lax.broadcasted_iota(I32, logits.shape, 1)
    logits = jnp.where(lane < n_exp, logits, NEG)
    e = jnp.exp(logits - jnp.max(logits, axis=-1, keepdims=True))
    aff = e / jnp.sum(e, axis=-1, keepdims=True)
    aff_t_ref[...] = aff.T[:n_exp, :]


def _router(x, g, mod, rw_pad, n_exp, n_ctx_rows, lat_len):
    m, d = x.shape
    tm = 256
    mrow = functools.partial(_mod_row, tm=tm, n_ctx_rows=n_ctx_rows, lat_len=lat_len)
    return pl.pallas_call(
        functools.partial(_router_body, n_exp),
        grid=(m // tm,),
        in_specs=[pl.BlockSpec((tm, d), lambda i: (i, 0)),
                  pl.BlockSpec((1, d), lambda i: (0, 0)),
                  pl.BlockSpec((1, 1, d), lambda i: (mrow(i) * N_MOD + 4, 0, 0)),
                  pl.BlockSpec((1, 1, d), lambda i: (mrow(i) * N_MOD + 3, 0, 0)),
                  pl.BlockSpec((d, LANES), lambda i: (0, 0))],
        out_specs=[pl.BlockSpec((tm, d), lambda i: (i, 0)),
                   pl.BlockSpec((n_exp, tm), lambda i: (0, i))],
        out_shape=[jax.ShapeDtypeStruct((m, d), BF16), jax.ShapeDtypeStruct((n_exp, m), F32)],
        compiler_params=_cparams(1, 40),
        name="router",
    )(x, g.reshape(1, d), mod, mod, rw_pad)


def _prefix_excl(mask):
    rows, t = mask.shape
    blk = min(t, 256)
    mb = jnp.where(mask, 1.0, 0.0)
    upper = jnp.where(lax.broadcasted_iota(I32, (blk, blk), 0) < lax.broadcasted_iota(I32, (blk, blk), 1), 1.0, 0.0)
    upper = upper.astype(BF16)
    carry = jnp.zeros((rows, 1), F32)
    outs = []
    for j in range(t // blk):
        part = mb[:, j * blk:(j + 1) * blk]
        outs.append(jnp.dot(part.astype(BF16), upper, preferred_element_type=F32) + carry)
        carry = carry + jnp.sum(part, axis=1, keepdims=True)
    return outs[0] if len(outs) == 1 else jnp.concatenate(outs, axis=1)


def _route_body(cap, has_alias, *refs):
    aff_ref, hn_ref = refs[:2]
    xe_ref, gs_ref, pos_t_ref, pos_ref = refs[-4:]
    e = pl.program_id(1)
    n_exp, t = aff_ref.shape

    @pl.when(e == 0)
    def _():
        a = aff_ref[...]
        bits = lax.bitcast_convert_type(a, I32)
        thr = jnp.zeros((n_exp, 1), I32)
        for bit in range(30, -1, -1):
            cand = thr | (1 << bit)
            cnt = jnp.sum(jnp.where(bits >= cand, 1.0, 0.0), axis=1, keepdims=True)
            thr = jnp.where(cnt >= cap, cand, thr)
        gt = bits > thr
        eq = bits == thr
        need = cap - jnp.sum(jnp.where(gt, 1.0, 0.0), axis=1, keepdims=True)
        sel = gt | (eq & (_prefix_excl(eq) < need))
        pos = jnp.where(sel, _prefix_excl(sel), -1.0)
        pos_ref[...] = pos
        pad = jnp.full((LANES - n_exp, t), -1.0, F32)
        pos_t_ref[...] = jnp.concatenate([pos, pad], axis=0).T

    prow = pos_ref[pl.ds(e, 1), :]
    slot = lax.broadcasted_iota(I32, (cap, t), 0).astype(F32)
    onehot = prow == slot
    xe_ref[0] = jnp.dot(jnp.where(onehot, 1.0, 0.0).astype(BF16), hn_ref[...],
                        preferred_element_type=F32).astype(BF16)
    arow = aff_ref[pl.ds(e, 1), :]
    gs = jnp.sum(jnp.where(onehot, arow, 0.0), axis=1, keepdims=True)
    gs_ref[0] = jnp.broadcast_to(gs, (cap, LANES))


def _route_gather(aff_t, hn, prev, n_exp, n_seq, t, row0, slot0, slots_total):
    m, d = hn.shape
    cap = EC_CAPACITY_FACTOR * t // n_exp
    rb0, sb0 = row0 // t, slot0 // cap
    in_specs = [pl.BlockSpec((n_exp, t), lambda b, e: (0, rb0 + b)),
                pl.BlockSpec((t, d), lambda b, e: (rb0 + b, 0))]
    args = [aff_t, hn]
    aliases = {}
    if prev is not None:
        in_specs += [pl.BlockSpec(memory_space=pl.ANY), pl.BlockSpec(memory_space=pl.ANY)]
        args += list(prev)
        aliases = {2: 0, 3: 1}
    return pl.pallas_call(
        functools.partial(_route_body, cap, prev is not None),
        grid=(n_seq, n_exp),
        in_specs=in_specs,
        out_specs=[pl.BlockSpec((1, cap, d), lambda b, e: (e, sb0 + b, 0)),
                   pl.BlockSpec((1, cap, LANES), lambda b, e: (e, sb0 + b, 0)),
                   pl.BlockSpec((t, LANES), lambda b, e: (b, 0))],
        out_shape=[jax.ShapeDtypeStruct((n_exp, slots_total, d), BF16),
                   jax.ShapeDtypeStruct((n_exp, slots_total, LANES), F32),
                   jax.ShapeDtypeStruct((n_seq * t, LANES), F32)],
        scratch_shapes=[pltpu.VMEM((n_exp, t), F32)],
        input_output_aliases=aliases,
        compiler_params=_cparams(2, 48),
        name="route_gather",
    )(*args)


FFN_ROWS = 512


def _ffn_up_body(xe_ref, wg_ref, wu_ref, o_ref, wgb_ref, wub_ref):
    wgb_ref[...] = wg_ref[0, 0].astype(BF16)
    wub_ref[...] = wu_ref[0, 0].astype(BF16)

    def rows(r, carry):
        sl = pl.ds(pl.multiple_of(r * FFN_ROWS, FFN_ROWS), FFN_ROWS)
        x = xe_ref[0, sl, :]
        gte = jnp.dot(x, wgb_ref[...], preferred_element_type=F32)
        up = jnp.dot(x, wub_ref[...], preferred_element_type=F32)
        o_ref[0, sl, :] = (_silu(gte) * up).astype(BF16)
        return carry

    lax.fori_loop(0, xe_ref.shape[1] // FFN_ROWS, rows, 0)


def _ffn_down_body(h_ref, wd_ref, gs_ref, o_ref, wdb_ref):
    wdb_ref[...] = wd_ref[0, 0].astype(BF16)

    def rows(r, carry):
        sl = pl.ds(pl.multiple_of(r * FFN_ROWS, FFN_ROWS), FFN_ROWS)
        y = jnp.dot(h_ref[0, sl, :], wdb_ref[...], preferred_element_type=F32)
        o_ref[0, sl, :] = (y * gs_ref[0, sl, 0:1]).astype(BF16)
        return carry

    lax.fori_loop(0, h_ref.shape[1] // FFN_ROWS, rows, 0)


def _expert_ffn(xe, gs, w_gate, w_up, w_down, layer):
    n_exp, slots, d = xe.shape
    ff = w_gate.shape[3]
    tf, tn = 256, 512
    hid = pl.pallas_call(
        _ffn_up_body,
        grid=(n_exp, ff // tf),
        in_specs=[pl.BlockSpec((1, slots, d), lambda e, f: (e, 0, 0)),
                  pl.BlockSpec((1, 1, d, tf), lambda e, f: (layer, e, 0, f)),
                  pl.BlockSpec((1, 1, d, tf), lambda e, f: (layer, e, 0, f))],
        out_specs=pl.BlockSpec((1, slots, tf), lambda e, f: (e, 0, f)),
        out_shape=jax.ShapeDtypeStruct((n_exp, slots, ff), BF16),
        scratch_shapes=[pltpu.VMEM((d, tf), BF16), pltpu.VMEM((d, tf), BF16)],
        compiler_params=_cparams(2, 48),
        name="ffn_up",
    )(xe, w_gate, w_up)
    return pl.pallas_call(
        _ffn_down_body,
        grid=(n_exp, d // tn),
        in_specs=[pl.BlockSpec((1, slots, ff), lambda e, j: (e, 0, 0)),
                  pl.BlockSpec((1, 1, ff, tn), lambda e, j: (layer, e, 0, j)),
                  pl.BlockSpec((1, slots, LANES), lambda e, j: (e, 0, 0))],
        out_specs=pl.BlockSpec((1, slots, tn), lambda e, j: (e, 0, j)),
        out_shape=jax.ShapeDtypeStruct((n_exp, slots, d), BF16),
        scratch_shapes=[pltpu.VMEM((ff, tn), BF16)],
        compiler_params=_cparams(2, 48),
        name="ffn_down",
    )(hid, w_down, gs)


def _combine_body(cap, has_alias, pos_t_ref, ye_ref, x_ref, g_ref, *rest):
    o_ref = rest[-1]
    n_exp = ye_ref.shape[0]
    pos_t = pos_t_ref[...]
    t = pos_t.shape[0]
    slot = lax.broadcasted_iota(I32, (t, cap), 1).astype(F32)
    acc = jnp.zeros(o_ref.shape, F32)
    for e in range(n_exp):
        onehot = pos_t[:, e:e + 1] == slot
        acc = acc + jnp.dot(jnp.where(onehot, 1.0, 0.0).astype(BF16), ye_ref[e], preferred_element_type=F32)
    o_ref[...] = x_ref[...] + g_ref[0] * acc


def _combine(pos_t, ye, x, mod, prev, n_seq, t, row0, slot0, mod_row0, mod_step):
    m, d = x.shape
    n_exp = ye.shape[0]
    cap = EC_CAPACITY_FACTOR * t // n_exp
    tn = 512
    rb0, sb0 = row0 // t, slot0 // cap
    in_specs = [pl.BlockSpec((t, LANES), lambda b, j: (b, 0)),
                pl.BlockSpec((n_exp, cap, tn), lambda b, j: (0, sb0 + b, j)),
                pl.BlockSpec((t, tn), lambda b, j: (rb0 + b, j)),
                pl.BlockSpec((1, 1, tn), lambda b, j: ((mod_row0 + mod_step * b) * N_MOD + 5, 0, j))]
    args = [pos_t, ye, x, mod]
    aliases = {}
    if prev is not None:
        in_specs.append(pl.BlockSpec(memory_space=pl.ANY))
        args.append(prev)
        aliases = {4: 0}
    return pl.pallas_call(
        functools.partial(_combine_body, cap, prev is not None),
        grid=(n_seq, d // tn),
        in_specs=in_specs,
        out_specs=pl.BlockSpec((t, tn), lambda b, j: (rb0 + b, j)),
        out_shape=jax.ShapeDtypeStruct((m, d), F32),
        input_output_aliases=aliases,
        compiler_params=_cparams(2, 48),
        name="combine",
    )(*args)


def _final_body(x_ref, g_ref, o_ref):
    x = x_ref[...]
    o_ref[...] = x * lax.rsqrt(jnp.mean(x * x, axis=-1, keepdims=True) + EPS) * g_ref[...]


def _final_norm(x, g, row0, rows):
    d = x.shape[1]
    tm = 512
    return pl.pallas_call(
        _final_body,
        grid=(rows // tm,),
        in_specs=[pl.BlockSpec((tm, d), lambda i: (row0 // tm + i, 0)),
                  pl.BlockSpec((1, d), lambda i: (0, 0))],
        out_specs=pl.BlockSpec((tm, d), lambda i: (i, 0)),
        out_shape=jax.ShapeDtypeStruct((rows, d), F32),
        compiler_params=_cparams(1, 40),
        name="final_norm",
    )(x, g.reshape(1, d))


def kernel(x_prompt, x_sample, state_ssd_fwd, state_ssd_bwd, c, c_ctx, w_ada, b_ada, norm_mix_g, norm_ffn_g, w_in, ssd_conv_w, ssd_conv_b, ssd_dt_bias, ssd_a_log, ssd_d, ssd_norm_g, conf_conv_w, conf_conv_b, conf_ln_g, conf_ln_b, sc_conv_w, w_out, router_w, w_gate, w_up, w_down, norm_final_g):
    nb, ctx_len, d = x_prompt.shape
    nd, lat_len, _ = x_sample.shape
    depth = w_in.shape[0]
    n_ctx_rows, n_lat_rows = nb * ctx_len, nd * lat_len
    heads = ssd_d.shape[1]
    hp = heads * SSD_HEAD_DIM
    gn = SSD_GROUPS * SSD_D_STATE
    conv_ch = hp + 2 * gn
    conf_w = conf_conv_w.shape[2]
    sc_w = sc_conv_w.shape[2]
    n_exp = router_w.shape[2]
    off_xbc = hp
    off_dt = off_xbc + conv_ch
    off_conf = off_dt + 2 * heads
    off_sc = off_conf + 2 * conf_w
    p_xbc, p_conf, p_sc = hp, hp + conv_ch, hp + conv_ch + 2 * conf_w

    x = jnp.concatenate([x_prompt.reshape(n_ctx_rows, d), x_sample.reshape(n_lat_rows, d)], axis=0)
    cond_rows = 16
    cond = jnp.concatenate([c_ctx[None, :], c, jnp.zeros((cond_rows - 1 - nd, d), F32)], axis=0)
    mod_all = _adaln_all(cond, w_ada, b_ada)

    st_f = state_ssd_fwd.reshape(nd, depth, hp, SSD_D_STATE)
    st_b = state_ssd_bwd.reshape(nd, depth, hp, SSD_D_STATE)
    ctx_cap = EC_CAPACITY_FACTOR * ctx_len // n_exp
    lat_cap = EC_CAPACITY_FACTOR * lat_len // n_exp
    slots_ctx = nb * ctx_cap
    slots_total = slots_ctx + nd * lat_cap
    zpad = jnp.zeros((d, LANES - heads), F32)
    lane_pad = lambda v: jnp.pad(v, ((0, 0), (0, LANES - heads))).reshape(2, 1, LANES)

    new_f, new_b = [], []
    for l in range(depth):
        mod = mod_all[l].reshape(cond_rows * N_MOD, 1, d)
        wl = w_in[l]
        w_main = jnp.concatenate([wl[:, :off_dt], wl[:, off_conf:]], axis=1).astype(BF16)
        w_dt = jnp.concatenate([wl[:, off_dt:off_dt + heads], zpad, wl[:, off_dt + heads:off_conf], zpad],
                               axis=1).astype(BF16)
        proj, dtraw = _inproj(x, norm_mix_g[l], mod, w_main, w_dt, n_ctx_rows, lat_len)

        xbc = _ssdconv(proj, p_xbc, ssd_conv_w[l], ssd_conv_b[l], n_ctx_rows, ctx_len, lat_len)
        bias, alog = lane_pad(ssd_dt_bias[l]), lane_pad(ssd_a_log[l])
        y2, fin = _ssd_scan(xbc, dtraw, bias, alog, None, None, l, nb, ctx_len, 0, True)
        (y2,) = _ssd_scan(xbc, dtraw, bias, alog, y2, (st_f, st_b), l, nd, lat_len, n_ctx_rows, False)
        new_f.append(fin[0].reshape(nb, heads, SSD_HEAD_DIM, SSD_D_STATE))
        new_b.append(fin[1].reshape(nb, heads, SSD_HEAD_DIM, SSD_D_STATE))
        ymix = _ssd_gate(y2, xbc, proj, jnp.repeat(ssd_d[l], SSD_HEAD_DIM), ssd_norm_g[l])
        u = _conformer(proj, p_conf, conf_conv_w[l], conf_conv_b[l], conf_ln_g[l], conf_ln_b[l], n_ctx_rows, ctx_len)
        v = _sconv(proj, p_sc, sc_conv_w[l], n_ctx_rows, ctx_len, lat_len)
        x = _outproj(ymix, u, v, w_out[l].astype(BF16), x, mod, n_ctx_rows, lat_len)

        rw_pad = jnp.pad(router_w[l], ((0, 0), (0, LANES - n_exp)))
        hn, aff_t = _router(x, norm_ffn_g[l], mod, rw_pad, n_exp, n_ctx_rows, lat_len)
        xe, gs, pos_c = _route_gather(aff_t, hn, None, n_exp, nb, ctx_len, 0, 0, slots_total)
        xe, gs, pos_l = _route_gather(aff_t, hn, (xe, gs), n_exp, nd, lat_len, n_ctx_rows, slots_ctx, slots_total)
        ye = _expert_ffn(xe, gs, w_gate, w_up, w_down, l)
        x2 = _combine(pos_c, ye, x, mod, None, nb, ctx_len, 0, 0, 0, 0)
        x = _combine(pos_l, ye, x, mod, x2, nd, lat_len, n_ctx_rows, slots_ctx, 1, 1)

    y_prompt = _final_norm(x, norm_final_g, 0, n_ctx_rows).reshape(nb, ctx_len, d)
    y_sample = _final_norm(x, norm_final_g, n_ctx_rows, n_lat_rows).reshape(nd, lat_len, d)
    return (y_prompt, y_sample, jnp.stack(new_f, axis=1), jnp.stack(new_b, axis=1))
```

```python
import functools

import jax
import jax.numpy as jnp
from jax import lax
from jax.experimental import pallas as pl
from jax.experimental.pallas import tpu as pltpu

F32 = jnp.float32
BF16 = jnp.bfloat16
I32 = jnp.int32
EPS = 1e-6
HIGHEST = lax.Precision.HIGHEST

GRID_W = 64
SSD_CHUNK = 128
SSD_HEAD_DIM = 64
SSD_GROUPS = 4
SSD_D_STATE = 128
N_MOD = 6
EC_CAPACITY_FACTOR = 2
LANES = 128
MIB = 1024 * 1024
NEG = -1e30
NORM_ROWS = 256


def _cparams(n_grid, vmem_mib):
    return pltpu.CompilerParams(dimension_semantics=("arbitrary",) * n_grid, vmem_limit_bytes=vmem_mib * MIB)


def _silu(x):
    return x * jax.nn.sigmoid(x)


def _softplus(x):
    y = jnp.exp(-jnp.abs(x))
    u = 1.0 + y
    um1 = u - 1.0
    l1p = jnp.where(um1 == 0.0, y, jnp.log(u) * (y / jnp.where(um1 == 0.0, 1.0, um1)))
    return jnp.maximum(x, 0.0) + l1p


def _mod_row(i, tm, n_ctx_rows, lat_len):
    nct = n_ctx_rows // tm
    per = lat_len // tm
    return jnp.where(i < nct, 0, 1 + (i - nct) // per)


def _adaln_body(cond_ref, w_ref, b_ref, o_ref):
    s = _silu(cond_ref[...]).astype(BF16)
    o_ref[0] = jnp.dot(s, w_ref[0].astype(BF16), preferred_element_type=F32) + b_ref[0]


def _adaln_all(cond, w_ada, b_ada):
    depth, d, n = w_ada.shape
    rows = cond.shape[0]
    tn = 1024
    return pl.pallas_call(
        _adaln_body,
        grid=(depth, n // tn),
        in_specs=[pl.BlockSpec((rows, d), lambda l, j: (0, 0)),
                  pl.BlockSpec((1, d, tn), lambda l, j: (l, 0, j)),
                  pl.BlockSpec((1, 1, tn), lambda l, j: (l, 0, j))],
        out_specs=pl.BlockSpec((1, rows, tn), lambda l, j: (l, 0, j)),
        out_shape=jax.ShapeDtypeStruct((depth, rows, n), F32),
        compiler_params=_cparams(2, 40),
        name="adaln",
    )(cond, w_ada, b_ada.reshape(depth, 1, n))


def _modnorm(x, g, sc, sh):
    r = lax.rsqrt(jnp.mean(x * x, axis=-1, keepdims=True) + EPS)
    return (x * r * g) * (1.0 + sc) + sh


def _inproj_body(x_ref, g_ref, sc_ref, sh_ref, w_ref, wdt_ref, o_ref, dt_ref, hn_ref):
    @pl.when(pl.program_id(1) == 0)
    def _():
        def rows(r, carry):
            sl = pl.ds(pl.multiple_of(r * NORM_ROWS, NORM_ROWS), NORM_ROWS)
            hb = _modnorm(x_ref[sl, :], g_ref[...], sc_ref[0], sh_ref[0]).astype(BF16)
            hn_ref[sl, :] = hb
            dt_ref[sl, :] = jnp.dot(hb, wdt_ref[...], preferred_element_type=F32)
            return carry

        lax.fori_loop(0, x_ref.shape[0] // NORM_ROWS, rows, 0)

    o_ref[...] = jnp.dot(hn_ref[...], w_ref[...], preferred_element_type=F32)


def _inproj(x, g, mod, w_main, w_dt, n_ctx_rows, lat_len):
    m, d = x.shape
    n = w_main.shape[1]
    ndt = w_dt.shape[1]
    tm, tn = 1024, 1024
    mrow = functools.partial(_mod_row, tm=tm, n_ctx_rows=n_ctx_rows, lat_len=lat_len)
    return pl.pallas_call(
        _inproj_body,
        grid=(m // tm, n // tn),
        in_specs=[pl.BlockSpec((tm, d), lambda i, j: (i, 0)),
                  pl.BlockSpec((1, d), lambda i, j: (0, 0)),
                  pl.BlockSpec((1, 1, d), lambda i, j: (mrow(i) * N_MOD + 1, 0, 0)),
                  pl.BlockSpec((1, 1, d), lambda i, j: (mrow(i) * N_MOD + 0, 0, 0)),
                  pl.BlockSpec((d, tn), lambda i, j: (0, j)),
                  pl.BlockSpec((d, ndt), lambda i, j: (0, 0))],
        out_specs=[pl.BlockSpec((tm, tn), lambda i, j: (i, j)),
                   pl.BlockSpec((tm, ndt), lambda i, j: (i, 0))],
        out_shape=[jax.ShapeDtypeStruct((m, n), F32), jax.ShapeDtypeStruct((m, ndt), F32)],
        scratch_shapes=[pltpu.VMEM((tm, d), BF16)],
        compiler_params=_cparams(2, 52),
        name="inproj",
    )(x, g.reshape(1, d), mod, mod, w_main, w_dt)


def _shift_rows(x, d, pos, seg):
    if d == 0:
        return x
    n = x.shape[0]
    y = pltpu.roll(x, (-d) % n, axis=0)
    ok = (pos >= -d) if d < 0 else (pos < seg - d)
    return jnp.where(ok, y, 0.0)


def _ssdconv_body(nct, seg_c, seg_l, x_ref, w_ref, b_ref, o_ref):
    seg = jnp.where(pl.program_id(0) < nct, seg_c, seg_l)
    x = x_ref[...]
    k = w_ref.shape[0]
    pos = lax.broadcasted_iota(I32, x.shape, 0) & (seg - 1)
    acc = jnp.broadcast_to(b_ref[...], x.shape)
    for j in range(k):
        acc = acc + _shift_rows(x, j - (k - 1) // 2, pos, seg) * w_ref[j:j + 1, :]
    o_ref[...] = _silu(acc)


def _ssdconv(proj, col0, w, b, n_ctx_rows, ctx_len, lat_len):
    m = proj.shape[0]
    k, ch = w.shape
    tr, tc = lat_len, 256
    cb0 = col0 // tc
    return pl.pallas_call(
        functools.partial(_ssdconv_body, n_ctx_rows // tr, ctx_len, lat_len),
        grid=(m // tr, ch // tc),
        in_specs=[pl.BlockSpec((tr, tc), lambda i, j: (i, cb0 + j)),
                  pl.BlockSpec((k, tc), lambda i, j: (0, j)),
                  pl.BlockSpec((1, tc), lambda i, j: (0, j))],
        out_specs=pl.BlockSpec((tr, tc), lambda i, j: (i, j)),
        out_shape=jax.ShapeDtypeStruct((m, ch), F32),
        compiler_params=_cparams(2, 40),
        name="ssdconv",
    )(proj, w, b.reshape(1, ch))


def _ssd_body(nc, has_h0, want_final, *refs):
    refs = list(refs)
    xs_ref, bm_ref, cm_ref, dt_ref, bias_ref, alog_ref = refs[:6]
    refs = refs[6:]
    if has_h0:
        h0f_ref, h0b_ref = refs[:2]
        refs = refs[3:]
    y_ref = refs[0]
    refs = refs[1:]
    if want_final:
        fin_ref = refs[0]
        refs = refs[1:]
    st_ref = refs[0]

    d = pl.program_id(1)
    c = pl.program_id(2)
    q = SSD_CHUNK
    p = SSD_HEAD_DIM
    nst = SSD_D_STATE
    heads_per_group = xs_ref.shape[1] // p // SSD_GROUPS

    @pl.when(c == 0)
    def _():
        if has_h0:
            @pl.when(d == 0)
            def _():
                st_ref[...] = h0f_ref[0, 0].T

            @pl.when(d == 1)
            def _():
                st_ref[...] = h0b_ref[0, 0].T
        else:
            st_ref[...] = jnp.zeros(st_ref.shape, F32)

    x = xs_ref[...]
    bmat = bm_ref[...]
    cmat = cm_ref[...]
    dt = _softplus(dt_ref[...] + bias_ref[0])
    a = -jnp.exp(alog_ref[0])
    dta = dt * a
    row = lax.broadcasted_iota(I32, (q, q), 0)
    col = lax.broadcasted_iota(I32, (q, q), 1)
    msk = (row - col) * (1 - 2 * d) >= 0
    acum = jnp.dot(jnp.where(msk, 1.0, 0.0), dta, precision=HIGHEST, preferred_element_type=F32)
    total = jnp.sum(dta, axis=0, keepdims=True)
    wst = jnp.exp(total - acum) * dt
    cd = jnp.exp(total)
    acum_t = acum.T
    dt_t = dt.T

    lo = lax.broadcasted_iota(I32, (q, 2 * p), 1) < p
    lo_row = lo[0:1, :]
    gw = heads_per_group * p
    for g in range(SSD_GROUPS):
        cg = cmat[:, g * nst:(g + 1) * nst]
        bg = bmat[:, g * nst:(g + 1) * nst]
        bg_t = bg.T.astype(BF16)
        cb = lax.dot_general(cg.astype(BF16), bg.astype(BF16), (((1,), (1,)), ((), ())),
                             preferred_element_type=F32)
        stg = st_ref[:, g * gw:(g + 1) * gw]
        xdt_parts, cd_parts = [], []
        for jj in range(heads_per_group // 2):
            h0 = g * heads_per_group + 2 * jj
            sl = slice(h0 * p, (h0 + 2) * p)
            w_parts, ce_parts = [], []
            for h in (h0, h0 + 1):
                acol = jnp.broadcast_to(acum[:, h:h + 1], (q, q))
                lm = jnp.exp(jnp.where(msk, acol - acum_t[h:h + 1, :], NEG))
                w_parts.append((cb * lm * dt_t[h:h + 1, :]).astype(BF16))
                ce_parts.append((cg * jnp.exp(acol)).astype(BF16))
            lhs_parts = w_parts + ce_parts
            xs = x[:, sl]
            sts = stg[:, 2 * jj * p:(2 * jj + 2) * p]
            rhs = jnp.concatenate([jnp.where(lo, xs, 0.0).astype(BF16), jnp.where(lo, 0.0, xs).astype(BF16),
                                   jnp.where(lo, sts, 0.0).astype(BF16), jnp.where(lo, 0.0, sts).astype(BF16)],
                                  axis=0)
            y_ref[0, :, sl] = jnp.dot(jnp.concatenate(lhs_parts, axis=1), rhs, preferred_element_type=F32)
            wpair = jnp.take_along_axis(wst, jnp.where(lo, h0, h0 + 1), axis=1)
            xdt_parts.append((xs * wpair).astype(BF16))
            cd_parts.append(jnp.where(lo_row, cd[:, h0:h0 + 1], cd[:, h0 + 1:h0 + 2]))
        new = jnp.dot(bg_t, jnp.concatenate(xdt_parts, axis=1), preferred_element_type=F32)
        st_ref[:, g * gw:(g + 1) * gw] = stg * jnp.concatenate(cd_parts, axis=1) + new

    if want_final:
        @pl.when(c == nc - 1)
        def _():
            fin_ref[0, 0] = st_ref[...].T


def _ssd_scan(xbc, dtraw, bias, alog, y_prev, h0, layer, n_seq, seq_len, row0, want_final):
    m = xbc.shape[0]
    q = SSD_CHUNK
    nc = seq_len // q
    gn = SSD_GROUPS * SSD_D_STATE
    hp = xbc.shape[1] - 2 * gn
    blk0 = row0 // q

    def rb(s, d, c):
        return blk0 + s * nc + c + d * (nc - 1 - 2 * c)

    in_specs = [pl.BlockSpec((q, hp), lambda s, d, c: (rb(s, d, c), 0)),
                pl.BlockSpec((q, gn), lambda s, d, c: (rb(s, d, c), hp // gn)),
                pl.BlockSpec((q, gn), lambda s, d, c: (rb(s, d, c), hp // gn + 1)),
                pl.BlockSpec((q, LANES), lambda s, d, c: (rb(s, d, c), d)),
                pl.BlockSpec((1, 1, LANES), lambda s, d, c: (d, 0, 0)),
                pl.BlockSpec((1, 1, LANES), lambda s, d, c: (d, 0, 0))]
    args = [xbc, xbc, xbc, dtraw, bias, alog]
    aliases = {}
    if h0 is not None:
        in_specs += [pl.BlockSpec((1, 1, hp, SSD_D_STATE), lambda s, d, c: (s, layer, 0, 0)),
                     pl.BlockSpec((1, 1, hp, SSD_D_STATE), lambda s, d, c: (s, layer, 0, 0))]
        args += [h0[0], h0[1]]
    if y_prev is not None:
        in_specs.append(pl.BlockSpec(memory_space=pl.ANY))
        args.append(y_prev)
        aliases = {len(args) - 1: 0}
    out_specs = [pl.BlockSpec((1, q, hp), lambda s, d, c: (d, rb(s, d, c), 0))]
    out_shape = [jax.ShapeDtypeStruct((2, m, hp), F32)]
    if want_final:
        out_specs.append(pl.BlockSpec((1, 1, hp, SSD_D_STATE), lambda s, d, c: (d, s, 0, 0)))
        out_shape.append(jax.ShapeDtypeStruct((2, n_seq, hp, SSD_D_STATE), F32))
    has_h0 = h0 is not None
    assert has_h0 == (y_prev is not None)
    return pl.pallas_call(
        functools.partial(_ssd_body, nc, has_h0, want_final),
        grid=(n_seq, 2, nc),
        in_specs=in_specs,
        out_specs=out_specs,
        out_shape=out_shape,
        scratch_shapes=[pltpu.VMEM((SSD_D_STATE, hp), F32)],
        input_output_aliases=aliases,
        compiler_params=_cparams(3, 40),
        name="ssd_scan",
    )(*args)


def _gate_body(yf_ref, yb_ref, xs_ref, z_ref, dexp_ref, g_ref, o_ref):
    y = yf_ref[0] + yb_ref[0] + dexp_ref[...] * xs_ref[...]
    v = y * _silu(z_ref[...])
    r = lax.rsqrt(jnp.mean(v * v, axis=-1, keepdims=True) + EPS)
    o_ref[...] = (v * r * g_ref[...]).astype(BF16)


def _ssd_gate(y2, xbc, proj, dexp, g):
    m = proj.shape[0]
    hp = y2.shape[2]
    tm = 256
    return pl.pallas_call(
        _gate_body,
        grid=(m // tm,),
        in_specs=[pl.BlockSpec((1, tm, hp), lambda i: (0, i, 0)),
                  pl.BlockSpec((1, tm, hp), lambda i: (1, i, 0)),
                  pl.BlockSpec((tm, hp), lambda i: (i, 0)),
                  pl.BlockSpec((tm, hp), lambda i: (i, 0)),
                  pl.BlockSpec((1, hp), lambda i: (0, 0)),
                  pl.BlockSpec((1, hp), lambda i: (0, 0))],
        out_specs=pl.BlockSpec((tm, hp), lambda i: (i, 0)),
        out_shape=jax.ShapeDtypeStruct((m, hp), BF16),
        compiler_params=_cparams(1, 40),
        name="ssd_gate",
    )(y2, y2, xbc, proj, dexp.reshape(1, hp), g.reshape(1, hp))


def _conf_body(nct, seg_c, seg_l, val_ref, gt_ref, w_ref, b_ref, lg_ref, lb_ref, o_ref):
    seg = jnp.where(pl.program_id(0) < nct, seg_c, seg_l)
    u = val_ref[...] * jax.nn.sigmoid(gt_ref[...])
    k = w_ref.shape[0]
    pos = lax.broadcasted_iota(I32, u.shape, 0) & (seg - 1)
    acc = jnp.broadcast_to(b_ref[...], u.shape)
    for j in range(k):
        acc = acc + _shift_rows(u, j - (k - 1) // 2, pos, seg) * w_ref[j:j + 1, :]
    mu = jnp.mean(acc, axis=-1, keepdims=True)
    cen = acc - mu
    var = jnp.mean(cen * cen, axis=-1, keepdims=True)
    y = cen * lax.rsqrt(var + EPS) * lg_ref[...] + lb_ref[...]
    o_ref[...] = _silu(y).astype(BF16)


def _conformer(proj, col0, w, b, lg, lb, n_ctx_rows, ctx_len):
    m = proj.shape[0]
    k, ch = w.shape
    tm = ctx_len
    cb0 = col0 // ch
    vec = lambda v: v.reshape(1, ch)
    vspec = pl.BlockSpec((1, ch), lambda i: (0, 0))
    return pl.pallas_call(
        functools.partial(_conf_body, n_ctx_rows // tm, ctx_len, GRID_W),
        grid=(m // tm,),
        in_specs=[pl.BlockSpec((tm, ch), lambda i: (i, cb0)),
                  pl.BlockSpec((tm, ch), lambda i: (i, cb0 + 1)),
                  pl.BlockSpec((k, ch), lambda i: (0, 0)),
                  vspec, vspec, vspec],
        out_specs=pl.BlockSpec((tm, ch), lambda i: (i, 0)),
        out_shape=jax.ShapeDtypeStruct((m, ch), BF16),
        compiler_params=_cparams(1, 40),
        name="conformer",
    )(proj, proj, w, vec(b), vec(lg), vec(lb))


def _sconv_body(nct, seg_c, bg_ref, cg_ref, hx_ref, w_ref, o_ref):
    i = pl.program_id(0)
    v = cg_ref[...] * hx_ref[...]
    n = v.shape[0]
    it = lax.broadcasted_iota(I32, v.shape, 0)
    w0, w1, w2 = w_ref[0:1, :], w_ref[1:2, :], w_ref[2:3, :]

    @pl.when(i < nct)
    def _():
        pos = it & (seg_c - 1)
        y = w1 * v + w0 * _shift_rows(v, -1, pos, seg_c) + w2 * _shift_rows(v, 1, pos, seg_c)
        o_ref[...] = (bg_ref[...] * y).astype(BF16)

    @pl.when(i >= nct)
    def _():
        y = w1 * v + w0 * _shift_rows(v, -GRID_W, it, n) + w2 * _shift_rows(v, GRID_W, it, n)
        o_ref[...] = (bg_ref[...] * y).astype(BF16)


def _sconv(proj, col0, w, n_ctx_rows, ctx_len, lat_len):
    m = proj.shape[0]
    k, ch = w.shape
    tr, tc = lat_len, 256
    cb0 = col0 // tc
    nch = ch // tc
    return pl.pallas_call(
        functools.partial(_sconv_body, n_ctx_rows // tr, ctx_len),
        grid=(m // tr, nch),
        in_specs=[pl.BlockSpec((tr, tc), lambda i, j: (i, cb0 + j)),
                  pl.BlockSpec((tr, tc), lambda i, j: (i, cb0 + nch + j)),
                  pl.BlockSpec((tr, tc), lambda i, j: (i, cb0 + 2 * nch + j)),
                  pl.BlockSpec((k, tc), lambda i, j: (0, j))],
        out_specs=pl.BlockSpec((tr, tc), lambda i, j: (i, j)),
        out_shape=jax.ShapeDtypeStruct((m, ch), BF16),
        compiler_params=_cparams(2, 40),
        name="sconv",
    )(proj, proj, proj, w)


def _outproj_body(y_ref, u_ref, v_ref, w_ref, x_ref, g_ref, o_ref):
    ky, ku = y_ref.shape[1], u_ref.shape[1]
    acc = jnp.dot(y_ref[...], w_ref[0:ky, :], preferred_element_type=F32)
    acc = acc + jnp.dot(u_ref[...], w_ref[ky:ky + ku, :], preferred_element_type=F32)
    acc = acc + jnp.dot(v_ref[...], w_ref[ky + ku:, :], preferred_element_type=F32)
    o_ref[...] = x_ref[...] + g_ref[0] * acc


def _outproj(y, u, v, w, x, mod, n_ctx_rows, lat_len):
    m, d = x.shape
    kt = w.shape[0]
    tm, tn = 1024, 512
    mrow = functools.partial(_mod_row, tm=tm, n_ctx_rows=n_ctx_rows, lat_len=lat_len)
    return pl.pallas_call(
        _outproj_body,
        grid=(m // tm, d // tn),
        in_specs=[pl.BlockSpec((tm, y.shape[1]), lambda i, j: (i, 0)),
                  pl.BlockSpec((tm, u.shape[1]), lambda i, j: (i, 0)),
                  pl.BlockSpec((tm, v.shape[1]), lambda i, j: (i, 0)),
                  pl.BlockSpec((kt, tn), lambda i, j: (0, j)),
                  pl.BlockSpec((tm, tn), lambda i, j: (i, j)),
                  pl.BlockSpec((1, 1, tn), lambda i, j: (mrow(i) * N_MOD + 2, 0, j))],
        out_specs=pl.BlockSpec((tm, tn), lambda i, j: (i, j)),
        out_shape=jax.ShapeDtypeStruct((m, d), F32),
        compiler_params=_cparams(2, 48),
        name="outproj",
    )(y, u, v, w, x, mod)


def _router_body(n_exp, x_ref, g_ref, sc_ref, sh_ref, rw_ref, hn_ref, aff_t_ref):
    hn = _modnorm(x_ref[...], g_ref[...], sc_ref[0], sh_ref[0])
    hn_ref[...] = hn.astype(BF16)
    logits = jnp.dot(hn, rw_ref[...], precision=HIGHEST, preferred_element_type=F32)
    lane = lax.broadcasted_iota(I32, logits.shape, 1)
    logits = jnp.where(lane < n_exp, logits, NEG)
    e = jnp.exp(logits - jnp.max(logits, axis=-1, keepdims=True))
    aff = e / jnp.sum(e, axis=-1, keepdims=True)
    aff_t_ref[...] = aff.T[:n_exp, :]


def _router(x, g, mod, rw_pad, n_exp, n_ctx_rows, lat_len):
    m, d = x.shape
    tm = 256
    mrow = functools.partial(_mod_row, tm=tm, n_ctx_rows=n_ctx_rows, lat_len=lat_len)
    return pl.pallas_call(
        functools.partial(_router_body, n_exp),
        grid=(m // tm,),
        in_specs=[pl.BlockSpec((tm, d), lambda i: (i, 0)),
                  pl.BlockSpec((1, d), lambda i: (0, 0)),
                  pl.BlockSpec((1, 1, d), lambda i: (mrow(i) * N_MOD + 4, 0, 0)),
                  pl.BlockSpec((1, 1, d), lambda i: (mrow(i) * N_MOD + 3, 0, 0)),
                  pl.BlockSpec((d, LANES), lambda i: (0, 0))],
        out_specs=[pl.BlockSpec((tm, d), lambda i: (i, 0)),
                   pl.BlockSpec((n_exp, tm), lambda i: (0, i))],
        out_shape=[jax.ShapeDtypeStruct((m, d), BF16), jax.ShapeDtypeStruct((n_exp, m), F32)],
        compiler_params=_cparams(1, 40),
        name="router",
    )(x, g.reshape(1, d), mod, mod, rw_pad)


def _prefix_excl(mask):
    rows, t = mask.shape
    blk = min(t, 256)
    mb = jnp.where(mask, 1.0, 0.0)
    upper = jnp.where(lax.broadcasted_iota(I32, (blk, blk), 0) < lax.broadcasted_iota(I32, (blk, blk), 1), 1.0, 0.0)
    upper = upper.astype(BF16)
    carry = jnp.zeros((rows, 1), F32)
    outs = []
    for j in range(t // blk):
        part = mb[:, j * blk:(j + 1) * blk]
        outs.append(jnp.dot(part.astype(BF16), upper, preferred_element_type=F32) + carry)
        carry = carry + jnp.sum(part, axis=1, keepdims=True)
    return outs[0] if len(outs) == 1 else jnp.concatenate(outs, axis=1)


ROUTE_EXPERTS = 4
ROUTE_COLS = 1024


def _route_body(cap, has_alias, *refs):
    aff_ref, hn_ref = refs[:2]
    xe_ref, gs_ref, pos_t_ref, pos_ref = refs[-4:]
    e = pl.program_id(1)
    n_exp, t = aff_ref.shape

    @pl.when(e == 0)
    def _():
        a = aff_ref[...]
        bits = lax.bitcast_convert_type(a, I32)
        thr = jnp.zeros((n_exp, 1), I32)
        for bit in range(30, -1, -1):
            cand = thr | (1 << bit)
            cnt = jnp.sum(jnp.where(bits >= cand, 1.0, 0.0), axis=1, keepdims=True)
            thr = jnp.where(cnt >= cap, cand, thr)
        gt = bits > thr
        eq = bits == thr
        need = cap - jnp.sum(jnp.where(gt, 1.0, 0.0), axis=1, keepdims=True)
        sel = gt | (eq & (_prefix_excl(eq) < need))
        pos = jnp.where(sel, _prefix_excl(sel), -1.0)
        pos_ref[...] = pos
        pad = jnp.full((LANES - n_exp, t), -1.0, F32)
        pos_t_ref[...] = jnp.concatenate([pos, pad], axis=0).T

    slot = lax.broadcasted_iota(I32, (cap, t), 0).astype(F32)
    pieces = []
    for k in range(ROUTE_EXPERTS):
        ek = e * ROUTE_EXPERTS + k
        onehot = pos_ref[pl.ds(ek, 1), :] == slot
        pieces.append(jnp.where(onehot, 1.0, 0.0).astype(BF16))
        gs = jnp.sum(jnp.where(onehot, aff_ref[pl.ds(ek, 1), :], 0.0), axis=1, keepdims=True)
        gs_ref[k] = jnp.broadcast_to(gs, (cap, LANES))
    sel = jnp.concatenate(pieces, axis=0)
    d = hn_ref.shape[1]
    for n0 in range(0, d, ROUTE_COLS):
        got = jnp.dot(sel, hn_ref[:, n0:n0 + ROUTE_COLS], preferred_element_type=F32).astype(BF16)
        for k in range(ROUTE_EXPERTS):
            xe_ref[k, :, n0:n0 + ROUTE_COLS] = got[k * cap:(k + 1) * cap, :]


def _route_gather(aff_t, hn, prev, n_exp, n_seq, t, row0, slot0, slots_total):
    m, d = hn.shape
    cap = EC_CAPACITY_FACTOR * t // n_exp
    rb0, sb0 = row0 // t, slot0 // cap
    in_specs = [pl.BlockSpec((n_exp, t), lambda b, e: (0, rb0 + b)),
                pl.BlockSpec((t, d), lambda b, e: (rb0 + b, 0))]
    args = [aff_t, hn]
    aliases = {}
    if prev is not None:
        in_specs += [pl.BlockSpec(memory_space=pl.ANY), pl.BlockSpec(memory_space=pl.ANY)]
        args += list(prev)
        aliases = {2: 0, 3: 1}
    eg = ROUTE_EXPERTS
    return pl.pallas_call(
        functools.partial(_route_body, cap, prev is not None),
        grid=(n_seq, n_exp // eg),
        in_specs=in_specs,
        out_specs=[pl.BlockSpec((eg, cap, d), lambda b, e: (e, sb0 + b, 0)),
                   pl.BlockSpec((eg, cap, LANES), lambda b, e: (e, sb0 + b, 0)),
                   pl.BlockSpec((t, LANES), lambda b, e: (b, 0))],
        out_shape=[jax.ShapeDtypeStruct((n_exp, slots_total, d), BF16),
                   jax.ShapeDtypeStruct((n_exp, slots_total, LANES), F32),
                   jax.ShapeDtypeStruct((n_seq * t, LANES), F32)],
        scratch_shapes=[pltpu.VMEM((n_exp, t), F32)],
        input_output_aliases=aliases,
        compiler_params=_cparams(2, 48),
        name="route_gather",
    )(*args)


FFN_ROWS = 512


def _ffn_up_body(xe_ref, wg_ref, wu_ref, o_ref, wgb_ref, wub_ref):
    wgb_ref[...] = wg_ref[0, 0].astype(BF16)
    wub_ref[...] = wu_ref[0, 0].astype(BF16)

    def rows(r, carry):
        sl = pl.ds(pl.multiple_of(r * FFN_ROWS, FFN_ROWS), FFN_ROWS)
        x = xe_ref[0, sl, :]
        gte = jnp.dot(x, wgb_ref[...], preferred_element_type=F32)
        up = jnp.dot(x, wub_ref[...], preferred_element_type=F32)
        o_ref[0, sl, :] = (_silu(gte) * up).astype(BF16)
        return carry

    lax.fori_loop(0, xe_ref.shape[1] // FFN_ROWS, rows, 0)


def _ffn_down_body(h_ref, wd_ref, gs_ref, o_ref, wdb_ref):
    wdb_ref[...] = wd_ref[0, 0].astype(BF16)

    def rows(r, carry):
        sl = pl.ds(pl.multiple_of(r * FFN_ROWS, FFN_ROWS), FFN_ROWS)
        y = jnp.dot(h_ref[0, sl, :], wdb_ref[...], preferred_element_type=F32)
        o_ref[0, sl, :] = (y * gs_ref[0, sl, 0:1]).astype(BF16)
        return carry

    lax.fori_loop(0, h_ref.shape[1] // FFN_ROWS, rows, 0)


def _expert_ffn(xe, gs, w_gate, w_up, w_down, layer):
    n_exp, slots, d = xe.shape
    ff = w_gate.shape[3]
    tf, tn = 256, 512
    hid = pl.pallas_call(
        _ffn_up_body,
        grid=(n_exp, ff // tf),
        in_specs=[pl.BlockSpec((1, slots, d), lambda e, f: (e, 0, 0)),
                  pl.BlockSpec((1, 1, d, tf), lambda e, f: (layer, e, 0, f)),
                  pl.BlockSpec((1, 1, d, tf), lambda e, f: (layer, e, 0, f))],
        out_specs=pl.BlockSpec((1, slots, tf), lambda e, f: (e, 0, f)),
        out_shape=jax.ShapeDtypeStruct((n_exp, slots, ff), BF16),
        scratch_shapes=[pltpu.VMEM((d, tf), BF16), pltpu.VMEM((d, tf), BF16)],
        compiler_params=_cparams(2, 48),
        name="ffn_up",
    )(xe, w_gate, w_up)
    return pl.pallas_call(
        _ffn_down_body,
        grid=(n_exp, d // tn),
        in_specs=[pl.BlockSpec((1, slots, ff), lambda e, j: (e, 0, 0)),
                  pl.BlockSpec((1, 1, ff, tn), lambda e, j: (layer, e, 0, j)),
                  pl.BlockSpec((1, slots, LANES), lambda e, j: (e, 0, 0))],
        out_specs=pl.BlockSpec((1, slots, tn), lambda e, j: (e, 0, j)),
        out_shape=jax.ShapeDtypeStruct((n_exp, slots, d), BF16),
        scratch_shapes=[pltpu.VMEM((ff, tn), BF16)],
        compiler_params=_cparams(2, 48),
        name="ffn_down",
    )(hid, w_down, gs)


COMBINE_COLS = 512


def _combine_body(cap, has_alias, pos_t_ref, ye_ref, x_ref, g_ref, *rest):
    o_ref, onehot_ref = rest[-2:]
    n_exp = ye_ref.shape[0]
    t, width = onehot_ref.shape

    @pl.when(pl.program_id(1) == 0)
    def _():
        pos_b = pos_t_ref[...].astype(BF16)
        for c0 in range(0, width, COMBINE_COLS):
            col = c0 + lax.broadcasted_iota(I32, (LANES, COMBINE_COLS), 1)
            expand = jnp.where(col // cap == lax.broadcasted_iota(I32, (LANES, COMBINE_COLS), 0), 1.0, 0.0)
            spread = jnp.dot(pos_b, expand.astype(BF16), preferred_element_type=F32)
            want = ((c0 + lax.broadcasted_iota(I32, (t, COMBINE_COLS), 1)) & (cap - 1)).astype(F32)
            onehot_ref[:, c0:c0 + COMBINE_COLS] = jnp.where(spread == want, 1.0, 0.0).astype(BF16)

    acc = jnp.dot(onehot_ref[...], ye_ref[...].reshape(n_exp * cap, ye_ref.shape[2]), preferred_element_type=F32)
    o_ref[...] = x_ref[...] + g_ref[0] * acc


def _combine(pos_t, ye, x, mod, prev, n_seq, t, row0, slot0, mod_row0, mod_step):
    m, d = x.shape
    n_exp = ye.shape[0]
    cap = EC_CAPACITY_FACTOR * t // n_exp
    tn = 256
    rb0, sb0 = row0 // t, slot0 // cap
    in_specs = [pl.BlockSpec((t, LANES), lambda b, j: (b, 0)),
                pl.BlockSpec((n_exp, cap, tn), lambda b, j: (0, sb0 + b, j)),
                pl.BlockSpec((t, tn), lambda b, j: (rb0 + b, j)),
                pl.BlockSpec((1, 1, tn), lambda b, j: ((mod_row0 + mod_step * b) * N_MOD + 5, 0, j))]
    args = [pos_t, ye, x, mod]
    aliases = {}
    if prev is not None:
        in_specs.append(pl.BlockSpec(memory_space=pl.ANY))
        args.append(prev)
        aliases = {4: 0}
    return pl.pallas_call(
        functools.partial(_combine_body, cap, prev is not None),
        grid=(n_seq, d // tn),
        in_specs=in_specs,
        out_specs=pl.BlockSpec((t, tn), lambda b, j: (rb0 + b, j)),
        out_shape=jax.ShapeDtypeStruct((m, d), F32),
        scratch_shapes=[pltpu.VMEM((t, n_exp * cap), BF16)],
        input_output_aliases=aliases,
        compiler_params=_cparams(2, 48),
        name="combine",
    )(*args)


def _final_body(x_ref, g_ref, o_ref):
    x = x_ref[...]
    o_ref[...] = x * lax.rsqrt(jnp.mean(x * x, axis=-1, keepdims=True) + EPS) * g_ref[...]


def _final_norm(x, g, row0, rows):
    d = x.shape[1]
    tm = 512
    return pl.pallas_call(
        _final_body,
        grid=(rows // tm,),
        in_specs=[pl.BlockSpec((tm, d), lambda i: (row0 // tm + i, 0)),
                  pl.BlockSpec((1, d), lambda i: (0, 0))],
        out_specs=pl.BlockSpec((tm, d), lambda i: (i, 0)),
        out_shape=jax.ShapeDtypeStruct((rows, d), F32),
        compiler_params=_cparams(1, 40),
        name="final_norm",
    )(x, g.reshape(1, d))


def kernel(x_prompt, x_sample, state_ssd_fwd, state_ssd_bwd, c, c_ctx, w_ada, b_ada, norm_mix_g, norm_ffn_g, w_in, ssd_conv_w, ssd_conv_b, ssd_dt_bias, ssd_a_log, ssd_d, ssd_norm_g, conf_conv_w, conf_conv_b, conf_ln_g, conf_ln_b, sc_conv_w, w_out, router_w, w_gate, w_up, w_down, norm_final_g):
    nb, ctx_len, d = x_prompt.shape
    nd, lat_len, _ = x_sample.shape
    depth = w_in.shape[0]
    n_ctx_rows, n_lat_rows = nb * ctx_len, nd * lat_len
    heads = ssd_d.shape[1]
    hp = heads * SSD_HEAD_DIM
    gn = SSD_GROUPS * SSD_D_STATE
    conv_ch = hp + 2 * gn
    conf_w = conf_conv_w.shape[2]
    sc_w = sc_conv_w.shape[2]
    n_exp = router_w.shape[2]
    off_xbc = hp
    off_dt = off_xbc + conv_ch
    off_conf = off_dt + 2 * heads
    off_sc = off_conf + 2 * conf_w
    p_xbc, p_conf, p_sc = hp, hp + conv_ch, hp + conv_ch + 2 * conf_w

    x = jnp.concatenate([x_prompt.reshape(n_ctx_rows, d), x_sample.reshape(n_lat_rows, d)], axis=0)
    cond_rows = 16
    cond = jnp.concatenate([c_ctx[None, :], c, jnp.zeros((cond_rows - 1 - nd, d), F32)], axis=0)
    mod_all = _adaln_all(cond, w_ada, b_ada)

    st_f = state_ssd_fwd.reshape(nd, depth, hp, SSD_D_STATE)
    st_b = state_ssd_bwd.reshape(nd, depth, hp, SSD_D_STATE)
    ctx_cap = EC_CAPACITY_FACTOR * ctx_len // n_exp
    lat_cap = EC_CAPACITY_FACTOR * lat_len // n_exp
    slots_ctx = nb * ctx_cap
    slots_total = slots_ctx + nd * lat_cap
    zpad = jnp.zeros((d, LANES - heads), F32)
    lane_pad = lambda v: jnp.pad(v, ((0, 0), (0, LANES - heads))).reshape(2, 1, LANES)

    new_f, new_b = [], []
    for l in range(depth):
        mod = mod_all[l].reshape(cond_rows * N_MOD, 1, d)
        wl = w_in[l]
        w_main = jnp.concatenate([wl[:, :off_dt], wl[:, off_conf:]], axis=1).astype(BF16)
        w_dt = jnp.concatenate([wl[:, off_dt:off_dt + heads], zpad, wl[:, off_dt + heads:off_conf], zpad],
                               axis=1).astype(BF16)
        proj, dtraw = _inproj(x, norm_mix_g[l], mod, w_main, w_dt, n_ctx_rows, lat_len)

        xbc = _ssdconv(proj, p_xbc, ssd_conv_w[l], ssd_conv_b[l], n_ctx_rows, ctx_len, lat_len)
        bias, alog = lane_pad(ssd_dt_bias[l]), lane_pad(ssd_a_log[l])
        y2, fin = _ssd_scan(xbc, dtraw, bias, alog, None, None, l, nb, ctx_len, 0, True)
        (y2,) = _ssd_scan(xbc, dtraw, bias, alog, y2, (st_f, st_b), l, nd, lat_len, n_ctx_rows, False)
        new_f.append(fin[0].reshape(nb, heads, SSD_HEAD_DIM, SSD_D_STATE))
        new_b.append(fin[1].reshape(nb, heads, SSD_HEAD_DIM, SSD_D_STATE))
        ymix = _ssd_gate(y2, xbc, proj, jnp.repeat(ssd_d[l], SSD_HEAD_DIM), ssd_norm_g[l])
        u = _conformer(proj, p_conf, conf_conv_w[l], conf_conv_b[l], conf_ln_g[l], conf_ln_b[l], n_ctx_rows, ctx_len)
        v = _sconv(proj, p_sc, sc_conv_w[l], n_ctx_rows, ctx_len, lat_len)
        x = _outproj(ymix, u, v, w_out[l].astype(BF16), x, mod, n_ctx_rows, lat_len)

        rw_pad = jnp.pad(router_w[l], ((0, 0), (0, LANES - n_exp)))
        hn, aff_t = _router(x, norm_ffn_g[l], mod, rw_pad, n_exp, n_ctx_rows, lat_len)
        xe, gs, pos_c = _route_gather(aff_t, hn, None, n_exp, nb, ctx_len, 0, 0, slots_total)
        xe, gs, pos_l = _route_gather(aff_t, hn, (xe, gs), n_exp, nd, lat_len, n_ctx_rows, slots_ctx, slots_total)
        ye = _expert_ffn(xe, gs, w_gate, w_up, w_down, l)
        x2 = _combine(pos_c, ye, x, mod, None, nb, ctx_len, 0, 0, 0, 0)
        x = _combine(pos_l, ye, x, mod, x2, nd, lat_len, n_ctx_rows, slots_ctx, 1, 1)

    y_prompt = _final_norm(x, norm_final_g, 0, n_ctx_rows).reshape(nb, ctx_len, d)
    y_sample = _final_norm(x, norm_final_g, n_ctx_rows, n_lat_rows).reshape(nd, lat_len, d)
    return (y_prompt, y_sample, jnp.stack(new_f, axis=1), jnp.stack(new_b, axis=1))
```

```python
import functools

import jax
import jax.numpy as jnp
from jax import lax
from jax.experimental import pallas as pl
from jax.experimental.pallas import tpu as pltpu

F32 = jnp.float32
BF16 = jnp.bfloat16
I32 = jnp.int32
EPS = 1e-6
HIGHEST = lax.Precision.HIGHEST

GRID_W = 64
SSD_CHUNK = 128
SSD_HEAD_DIM = 64
SSD_GROUPS = 4
SSD_D_STATE = 128
N_MOD = 6
EC_CAPACITY_FACTOR = 2
LANES = 128
SUBLANES = 8
MIB = 1024 * 1024
NEG = -1e30
NORM_ROWS = 256


def _cparams(n_grid, vmem_mib):
    return pltpu.CompilerParams(dimension_semantics=("arbitrary",) * n_grid, vmem_limit_bytes=vmem_mib * MIB)


def _silu(x):
    return x * jax.nn.sigmoid(x)


def _softplus(x):
    y = jnp.exp(-jnp.abs(x))
    u = 1.0 + y
    um1 = u - 1.0
    l1p = jnp.where(um1 == 0.0, y, jnp.log(u) * (y / jnp.where(um1 == 0.0, 1.0, um1)))
    return jnp.maximum(x, 0.0) + l1p


def _mod_row(i, tm, n_ctx_rows, lat_len):
    nct = n_ctx_rows // tm
    per = lat_len // tm
    return jnp.where(i < nct, 0, 1 + (i - nct) // per)


def _adaln_body(cond_ref, w_ref, b_ref, o_ref):
    s = _silu(cond_ref[...]).astype(BF16)
    o_ref[0] = jnp.dot(s, w_ref[0].astype(BF16), preferred_element_type=F32) + b_ref[0]


def _adaln_all(cond, w_ada, b_ada):
    depth, d, n = w_ada.shape
    rows = cond.shape[0]
    tn = 1024
    return pl.pallas_call(
        _adaln_body,
        grid=(depth, n // tn),
        in_specs=[pl.BlockSpec((rows, d), lambda l, j: (0, 0)),
                  pl.BlockSpec((1, d, tn), lambda l, j: (l, 0, j)),
                  pl.BlockSpec((1, 1, tn), lambda l, j: (l, 0, j))],
        out_specs=pl.BlockSpec((1, rows, tn), lambda l, j: (l, 0, j)),
        out_shape=jax.ShapeDtypeStruct((depth, rows, n), F32),
        compiler_params=_cparams(2, 40),
        name="adaln",
    )(cond, w_ada, b_ada.reshape(depth, 1, n))


def _modnorm(x, g, sc, sh):
    r = lax.rsqrt(jnp.mean(x * x, axis=-1, keepdims=True) + EPS)
    return (x * r * g) * (1.0 + sc) + sh


def _inproj_body(x_ref, g_ref, sc_ref, sh_ref, w_ref, wdt_ref, o_ref, dt_ref, hn_ref):
    @pl.when(pl.program_id(1) == 0)
    def _():
        def rows(r, carry):
            sl = pl.ds(pl.multiple_of(r * NORM_ROWS, NORM_ROWS), NORM_ROWS)
            hb = _modnorm(x_ref[sl, :], g_ref[...], sc_ref[0], sh_ref[0]).astype(BF16)
            hn_ref[sl, :] = hb
            dt_ref[sl, :] = jnp.dot(hb, wdt_ref[...], preferred_element_type=F32)
            return carry

        lax.fori_loop(0, x_ref.shape[0] // NORM_ROWS, rows, 0)

    o_ref[...] = jnp.dot(hn_ref[...], w_ref[...], preferred_element_type=F32)


def _inproj(x, g, mod, w_main, w_dt, n_ctx_rows, lat_len):
    m, d = x.shape
    n = w_main.shape[1]
    ndt = w_dt.shape[1]
    tm, tn = 1024, 1024
    mrow = functools.partial(_mod_row, tm=tm, n_ctx_rows=n_ctx_rows, lat_len=lat_len)
    return pl.pallas_call(
        _inproj_body,
        grid=(m // tm, n // tn),
        in_specs=[pl.BlockSpec((tm, d), lambda i, j: (i, 0)),
                  pl.BlockSpec((1, d), lambda i, j: (0, 0)),
                  pl.BlockSpec((1, 1, d), lambda i, j: (mrow(i) * N_MOD + 1, 0, 0)),
                  pl.BlockSpec((1, 1, d), lambda i, j: (mrow(i) * N_MOD + 0, 0, 0)),
                  pl.BlockSpec((d, tn), lambda i, j: (0, j)),
                  pl.BlockSpec((d, ndt), lambda i, j: (0, 0))],
        out_specs=[pl.BlockSpec((tm, tn), lambda i, j: (i, j)),
                   pl.BlockSpec((tm, ndt), lambda i, j: (i, 0))],
        out_shape=[jax.ShapeDtypeStruct((m, n), F32), jax.ShapeDtypeStruct((m, ndt), F32)],
        scratch_shapes=[pltpu.VMEM((tm, d), BF16)],
        compiler_params=_cparams(2, 52),
        name="inproj",
    )(x, g.reshape(1, d), mod, mod, w_main, w_dt)


def _shift_rows(x, d, pos, seg):
    if d == 0:
        return x
    n = x.shape[0]
    y = pltpu.roll(x, (-d) % n, axis=0)
    ok = (pos >= -d) if d < 0 else (pos < seg - d)
    return jnp.where(ok, y, 0.0)


def _ssdconv_body(nct, seg_c, seg_l, x_ref, w_ref, b_ref, o_ref):
    seg = jnp.where(pl.program_id(0) < nct, seg_c, seg_l)
    x = x_ref[...]
    k = w_ref.shape[0]
    pos = lax.broadcasted_iota(I32, x.shape, 0) & (seg - 1)
    acc = jnp.broadcast_to(b_ref[...], x.shape)
    for j in range(k):
        acc = acc + _shift_rows(x, j - (k - 1) // 2, pos, seg) * w_ref[j:j + 1, :]
    o_ref[...] = _silu(acc)


def _ssdconv(proj, col0, w, b, n_ctx_rows, ctx_len, lat_len):
    m = proj.shape[0]
    k, ch = w.shape
    tr, tc = lat_len, 256
    cb0 = col0 // tc
    return pl.pallas_call(
        functools.partial(_ssdconv_body, n_ctx_rows // tr, ctx_len, lat_len),
        grid=(m // tr, ch // tc),
        in_specs=[pl.BlockSpec((tr, tc), lambda i, j: (i, cb0 + j)),
                  pl.BlockSpec((k, tc), lambda i, j: (0, j)),
                  pl.BlockSpec((1, tc), lambda i, j: (0, j))],
        out_specs=pl.BlockSpec((tr, tc), lambda i, j: (i, j)),
        out_shape=jax.ShapeDtypeStruct((m, ch), F32),
        compiler_params=_cparams(2, 40),
        name="ssdconv",
    )(proj, w, b.reshape(1, ch))


def _ssd_body(nc, backward, has_h0, want_final, *refs):
    refs = list(refs)
    xs_ref, bm_ref, cm_ref, dt_ref, bias_ref, alog_ref = refs[:6]
    refs = refs[6:]
    if has_h0:
        h0_ref = refs[0]
        refs = refs[1:]
    if backward:
        yf_ref, z_ref, dexp_ref, gn_ref = refs[:4]
        refs = refs[4:]
    if has_h0:
        refs = refs[1:]
    y_ref = refs[0]
    refs = refs[1:]
    if want_final:
        fin_ref = refs[0]
        refs = refs[1:]
    st_ref = refs[0]
    if backward:
        yb_ref = refs[1]

    c = pl.program_id(1)
    q = SSD_CHUNK
    p = SSD_HEAD_DIM
    nst = SSD_D_STATE
    heads_per_group = xs_ref.shape[1] // p // SSD_GROUPS

    @pl.when(c == 0)
    def _():
        if has_h0:
            st_ref[...] = h0_ref[0, 0].T
        else:
            st_ref[...] = jnp.zeros(st_ref.shape, F32)

    x = xs_ref[...]
    bmat = bm_ref[...]
    cmat = cm_ref[...]
    dt = _softplus(dt_ref[...] + bias_ref[0])
    a = -jnp.exp(alog_ref[0])
    dta = dt * a
    row = lax.broadcasted_iota(I32, (q, q), 0)
    col = lax.broadcasted_iota(I32, (q, q), 1)
    msk = (col >= row) if backward else (col <= row)
    acum = jnp.dot(jnp.where(msk, 1.0, 0.0), dta, precision=HIGHEST, preferred_element_type=F32)
    total = jnp.sum(dta, axis=0, keepdims=True)
    wst = jnp.exp(total - acum) * dt
    cd = jnp.exp(total)
    acum_t = acum.T
    dt_t = dt.T

    lo = lax.broadcasted_iota(I32, (q, 2 * p), 1) < p
    lo_row = lo[0:1, :]
    gw = heads_per_group * p
    for g in range(SSD_GROUPS):
        cg = cmat[:, g * nst:(g + 1) * nst]
        bg = bmat[:, g * nst:(g + 1) * nst]
        bg_t = bg.T.astype(BF16)
        cb = lax.dot_general(cg.astype(BF16), bg.astype(BF16), (((1,), (1,)), ((), ())),
                             preferred_element_type=F32)
        stg = st_ref[:, g * gw:(g + 1) * gw]
        xdt_parts, cd_parts = [], []
        for jj in range(heads_per_group // 2):
            h0 = g * heads_per_group + 2 * jj
            sl = slice(h0 * p, (h0 + 2) * p)
            w_parts, ce_parts = [], []
            for h in (h0, h0 + 1):
                acol = jnp.broadcast_to(acum[:, h:h + 1], (q, q))
                lm = jnp.exp(jnp.where(msk, acol - acum_t[h:h + 1, :], NEG))
                w_parts.append((cb * lm * dt_t[h:h + 1, :]).astype(BF16))
                ce_parts.append((cg * jnp.exp(acol)).astype(BF16))
            lhs_parts = w_parts + ce_parts
            xs = x[:, sl]
            sts = stg[:, 2 * jj * p:(2 * jj + 2) * p]
            rhs = jnp.concatenate([jnp.where(lo, xs, 0.0).astype(BF16), jnp.where(lo, 0.0, xs).astype(BF16),
                                   jnp.where(lo, sts, 0.0).astype(BF16), jnp.where(lo, 0.0, sts).astype(BF16)],
                                  axis=0)
            y_slab = jnp.dot(jnp.concatenate(lhs_parts, axis=1), rhs, preferred_element_type=F32)
            if backward:
                yb_ref[:, sl] = y_slab
            else:
                y_ref[:, sl] = y_slab
            wpair = jnp.take_along_axis(wst, jnp.where(lo, h0, h0 + 1), axis=1)
            xdt_parts.append((xs * wpair).astype(BF16))
            cd_parts.append(jnp.where(lo_row, cd[:, h0:h0 + 1], cd[:, h0 + 1:h0 + 2]))
        new = jnp.dot(bg_t, jnp.concatenate(xdt_parts, axis=1), preferred_element_type=F32)
        st_ref[:, g * gw:(g + 1) * gw] = stg * jnp.concatenate(cd_parts, axis=1) + new

    if backward:
        v = (yf_ref[...] + yb_ref[...] + dexp_ref[...] * x) * _silu(z_ref[...])
        r = lax.rsqrt(jnp.mean(v * v, axis=-1, keepdims=True) + EPS)
        y_ref[...] = (v * r * gn_ref[...]).astype(BF16)

    if want_final:
        @pl.when(c == nc - 1)
        def _():
            fin_ref[0] = st_ref[...].T


def _ssd_scan(backward, xbc, dtraw, bias, alog, gate_in, prev, h0, layer, n_seq, seq_len, row0, want_final):
    m = xbc.shape[0]
    q = SSD_CHUNK
    nc = seq_len // q
    gn = SSD_GROUPS * SSD_D_STATE
    hp = xbc.shape[1] - 2 * gn
    blk0 = row0 // q
    d = 1 if backward else 0

    def rb(s, c):
        return blk0 + s * nc + (nc - 1 - c if backward else c)

    in_specs = [pl.BlockSpec((q, hp), lambda s, c: (rb(s, c), 0)),
                pl.BlockSpec((q, gn), lambda s, c: (rb(s, c), hp // gn)),
                pl.BlockSpec((q, gn), lambda s, c: (rb(s, c), hp // gn + 1)),
                pl.BlockSpec((q, LANES), lambda s, c: (rb(s, c), d)),
                pl.BlockSpec((1, 1, LANES), lambda s, c: (d, 0, 0)),
                pl.BlockSpec((1, 1, LANES), lambda s, c: (d, 0, 0))]
    args = [xbc, xbc, xbc, dtraw, bias, alog]
    aliases = {}
    if h0 is not None:
        in_specs.append(pl.BlockSpec((1, 1, hp, SSD_D_STATE), lambda s, c: (s, layer, 0, 0)))
        args.append(h0)
    if backward:
        y_f, proj, dexp, norm_g = gate_in
        vspec = pl.BlockSpec((1, hp), lambda s, c: (0, 0))
        in_specs += [pl.BlockSpec((q, hp), lambda s, c: (rb(s, c), 0)),
                     pl.BlockSpec((q, hp), lambda s, c: (rb(s, c), 0)), vspec, vspec]
        args += [y_f, proj, dexp.reshape(1, hp), norm_g.reshape(1, hp)]
    assert (h0 is not None) == (prev is not None)
    if prev is not None:
        in_specs.append(pl.BlockSpec(memory_space=pl.ANY))
        args.append(prev)
        aliases = {len(args) - 1: 0}
    out_specs = [pl.BlockSpec((q, hp), lambda s, c: (rb(s, c), 0))]
    out_shape = [jax.ShapeDtypeStruct((m, hp), BF16 if backward else F32)]
    if want_final:
        out_specs.append(pl.BlockSpec((1, hp, SSD_D_STATE), lambda s, c: (s, 0, 0)))
        out_shape.append(jax.ShapeDtypeStruct((n_seq, hp, SSD_D_STATE), F32))
    scratch = [pltpu.VMEM((SSD_D_STATE, hp), F32)]
    if backward:
        scratch.append(pltpu.VMEM((q, hp), F32))
    return pl.pallas_call(
        functools.partial(_ssd_body, nc, backward, h0 is not None, want_final),
        grid=(n_seq, nc),
        in_specs=in_specs,
        out_specs=out_specs,
        out_shape=out_shape,
        scratch_shapes=scratch,
        input_output_aliases=aliases,
        compiler_params=_cparams(2, 40),
        name="ssd_scan_bwd" if backward else "ssd_scan_fwd",
    )(*args)


CONV_PAD = 16
CONV_ROWS = 64
CONV_CH = 256


def _conf_conv(seg, val_ref, gt_ref, w_ref, b_ref, buf_ref, acc_ref):
    tm, ch = val_ref.shape
    k = w_ref.shape[0]
    half = (k - 1) // 2
    nseg = tm // seg
    stride = seg + CONV_PAD
    zero = jnp.zeros((CONV_PAD, ch), F32)
    for j in range(nseg + 1):
        buf_ref[0, j * stride:j * stride + CONV_PAD, :] = zero
    for j in range(nseg):
        rows = slice(j * seg, (j + 1) * seg)
        buf_ref[0, j * stride + CONV_PAD:(j + 1) * stride, :] = val_ref[rows, :] * jax.nn.sigmoid(gt_ref[rows, :])
    used = nseg * stride + CONV_PAD
    padded = buf_ref[0, 0:used, :]
    for s in range(1, SUBLANES):
        buf_ref[s, 0:used, :] = pltpu.roll(padded, used - s, axis=0)
    for r0 in range(0, tm, CONV_ROWS):
        base = (r0 // seg) * stride + CONV_PAD + (r0 % seg) - half
        for c0 in range(0, ch, CONV_CH):
            cols = slice(c0, c0 + CONV_CH)
            acc = jnp.broadcast_to(b_ref[:, cols], (CONV_ROWS, CONV_CH))
            for j in range(k):
                off = base + j
                al = off - off % SUBLANES
                acc = acc + buf_ref[off % SUBLANES, al:al + CONV_ROWS, cols] * w_ref[j:j + 1, cols]
            acc_ref[r0:r0 + CONV_ROWS, cols] = acc


def _conf_body(nct, seg_c, seg_l, val_ref, gt_ref, w_ref, b_ref, lg_ref, lb_ref, o_ref, buf_ref, acc_ref):
    i = pl.program_id(0)

    @pl.when(i < nct)
    def _():
        _conf_conv(seg_c, val_ref, gt_ref, w_ref, b_ref, buf_ref, acc_ref)

    @pl.when(i >= nct)
    def _():
        _conf_conv(seg_l, val_ref, gt_ref, w_ref, b_ref, buf_ref, acc_ref)

    acc = acc_ref[...]
    mu = jnp.mean(acc, axis=-1, keepdims=True)
    cen = acc - mu
    var = jnp.mean(cen * cen, axis=-1, keepdims=True)
    y = cen * lax.rsqrt(var + EPS) * lg_ref[...] + lb_ref[...]
    o_ref[...] = _silu(y).astype(BF16)


def _conformer(proj, col0, w, b, lg, lb, n_ctx_rows, ctx_len):
    m = proj.shape[0]
    k, ch = w.shape
    tm = ctx_len
    cb0 = col0 // ch
    vec = lambda v: v.reshape(1, ch)
    vspec = pl.BlockSpec((1, ch), lambda i: (0, 0))
    return pl.pallas_call(
        functools.partial(_conf_body, n_ctx_rows // tm, ctx_len, GRID_W),
        grid=(m // tm,),
        in_specs=[pl.BlockSpec((tm, ch), lambda i: (i, cb0)),
                  pl.BlockSpec((tm, ch), lambda i: (i, cb0 + 1)),
                  pl.BlockSpec((k, ch), lambda i: (0, 0)),
                  vspec, vspec, vspec],
        out_specs=pl.BlockSpec((tm, ch), lambda i: (i, 0)),
        out_shape=jax.ShapeDtypeStruct((m, ch), BF16),
        scratch_shapes=[pltpu.VMEM((SUBLANES, tm // GRID_W * (GRID_W + CONV_PAD) + CONV_PAD, ch), F32),
                        pltpu.VMEM((tm, ch), F32)],
        compiler_params=_cparams(1, 40),
        name="conformer",
    )(proj, proj, w, vec(b), vec(lg), vec(lb))


def _sconv_body(nct, seg_c, bg_ref, cg_ref, hx_ref, w_ref, o_ref):
    i = pl.program_id(0)
    v = cg_ref[...] * hx_ref[...]
    n = v.shape[0]
    it = lax.broadcasted_iota(I32, v.shape, 0)
    w0, w1, w2 = w_ref[0:1, :], w_ref[1:2, :], w_ref[2:3, :]

    @pl.when(i < nct)
    def _():
        pos = it & (seg_c - 1)
        y = w1 * v + w0 * _shift_rows(v, -1, pos, seg_c) + w2 * _shift_rows(v, 1, pos, seg_c)
        o_ref[...] = (bg_ref[...] * y).astype(BF16)

    @pl.when(i >= nct)
    def _():
        y = w1 * v + w0 * _shift_rows(v, -GRID_W, it, n) + w2 * _shift_rows(v, GRID_W, it, n)
        o_ref[...] = (bg_ref[...] * y).astype(BF16)


def _sconv(proj, col0, w, n_ctx_rows, ctx_len, lat_len):
    m = proj.shape[0]
    k, ch = w.shape
    tr, tc = lat_len, 256
    cb0 = col0 // tc
    nch = ch // tc
    return pl.pallas_call(
        functools.partial(_sconv_body, n_ctx_rows // tr, ctx_len),
        grid=(m // tr, nch),
        in_specs=[pl.BlockSpec((tr, tc), lambda i, j: (i, cb0 + j)),
                  pl.BlockSpec((tr, tc), lambda i, j: (i, cb0 + nch + j)),
                  pl.BlockSpec((tr, tc), lambda i, j: (i, cb0 + 2 * nch + j)),
                  pl.BlockSpec((k, tc), lambda i, j: (0, j))],
        out_specs=pl.BlockSpec((tr, tc), lambda i, j: (i, j)),
        out_shape=jax.ShapeDtypeStruct((m, ch), BF16),
        compiler_params=_cparams(2, 40),
        name="sconv",
    )(proj, proj, proj, w)


def _outproj_body(y_ref, u_ref, v_ref, w_ref, x_ref, g_ref, o_ref):
    ky, ku = y_ref.shape[1], u_ref.shape[1]
    acc = jnp.dot(y_ref[...], w_ref[0:ky, :], preferred_element_type=F32)
    acc = acc + jnp.dot(u_ref[...], w_ref[ky:ky + ku, :], preferred_element_type=F32)
    acc = acc + jnp.dot(v_ref[...], w_ref[ky + ku:, :], preferred_element_type=F32)
    o_ref[...] = x_ref[...] + g_ref[0] * acc


def _outproj(y, u, v, w, x, mod, n_ctx_rows, lat_len):
    m, d = x.shape
    kt = w.shape[0]
    tm, tn = 1024, 512
    mrow = functools.partial(_mod_row, tm=tm, n_ctx_rows=n_ctx_rows, lat_len=lat_len)
    return pl.pallas_call(
        _outproj_body,
        grid=(m // tm, d // tn),
        in_specs=[pl.BlockSpec((tm, y.shape[1]), lambda i, j: (i, 0)),
                  pl.BlockSpec((tm, u.shape[1]), lambda i, j: (i, 0)),
                  pl.BlockSpec((tm, v.shape[1]), lambda i, j: (i, 0)),
                  pl.BlockSpec((kt, tn), lambda i, j: (0, j)),
                  pl.BlockSpec((tm, tn), lambda i, j: (i, j)),
                  pl.BlockSpec((1, 1, tn), lambda i, j: (mrow(i) * N_MOD + 2, 0, j))],
        out_specs=pl.BlockSpec((tm, tn), lambda i, j: (i, j)),
        out_shape=jax.ShapeDtypeStruct((m, d), F32),
        compiler_params=_cparams(2, 48),
        name="outproj",
    )(y, u, v, w, x, mod)


def _router_body(n_exp, x_ref, g_ref, sc_ref, sh_ref, rw_ref, hn_ref, aff_t_ref):
    hn = _modnorm(x_ref[...], g_ref[...], sc_ref[0], sh_ref[0])
    hn_ref[...] = hn.astype(BF16)
    logits = jnp.dot(hn, rw_ref[...], precision=HIGHEST, preferred_element_type=F32)
    lane = lax.broadcasted_iota(I32, logits.shape, 1)
    logits = jnp.where(lane < n_exp, logits, NEG)
    e = jnp.exp(logits - jnp.max(logits, axis=-1, keepdims=True))
    aff = e / jnp.sum(e, axis=-1, keepdims=True)
    aff_t_ref[...] = aff.T[:n_exp, :]


def _router(x, g, mod, rw_pad, n_exp, n_ctx_rows, lat_len):
    m, d = x.shape
    tm = 256
    mrow = functools.partial(_mod_row, tm=tm, n_ctx_rows=n_ctx_rows, lat_len=lat_len)
    return pl.pallas_call(
        functools.partial(_router_body, n_exp),
        grid=(m // tm,),
        in_specs=[pl.BlockSpec((tm, d), lambda i: (i, 0)),
                  pl.BlockSpec((1, d), lambda i: (0, 0)),
                  pl.BlockSpec((1, 1, d), lambda i: (mrow(i) * N_MOD + 4, 0, 0)),
                  pl.BlockSpec((1, 1, d), lambda i: (mrow(i) * N_MOD + 3, 0, 0)),
                  pl.BlockSpec((d, LANES), lambda i: (0, 0))],
        out_specs=[pl.BlockSpec((tm, d), lambda i: (i, 0)),
                   pl.BlockSpec((n_exp, tm), lambda i: (0, i))],
        out_shape=[jax.ShapeDtypeStruct((m, d), BF16), jax.ShapeDtypeStruct((n_exp, m), F32)],
        compiler_params=_cparams(1, 40),
        name="router",
    )(x, g.reshape(1, d), mod, mod, rw_pad)


def _prefix_excl(mask):
    rows, t = mask.shape
    blk = min(t, 256)
    mb = jnp.where(mask, 1.0, 0.0)
    upper = jnp.where(lax.broadcasted_iota(I32, (blk, blk), 0) < lax.broadcasted_iota(I32, (blk, blk), 1), 1.0, 0.0)
    upper = upper.astype(BF16)
    carry = jnp.zeros((rows, 1), F32)
    outs = []
    for j in range(t // blk):
        part = mb[:, j * blk:(j + 1) * blk]
        outs.append(jnp.dot(part.astype(BF16), upper, preferred_element_type=F32) + carry)
        carry = carry + jnp.sum(part, axis=1, keepdims=True)
    return outs[0] if len(outs) == 1 else jnp.concatenate(outs, axis=1)


ROUTE_EXPERTS = 4
ROUTE_COLS = 1024


def _route_body(cap, has_alias, *refs):
    aff_ref, hn_ref = refs[:2]
    xe_ref, gs_ref, pos_t_ref, pos_ref = refs[-4:]
    e = pl.program_id(1)
    n_exp, t = aff_ref.shape

    @pl.when(e == 0)
    def _():
        a = aff_ref[...]
        bits = lax.bitcast_convert_type(a, I32)
        thr = jnp.zeros((n_exp, 1), I32)
        count_ge = lambda cand: jnp.sum(jnp.where(bits >= cand, 1.0, 0.0), axis=1, keepdims=True)
        for bit in range(29, 0, -2):
            c1, c2, c3 = thr | (1 << bit), thr | (2 << bit), thr | (3 << bit)
            n1, n2, n3 = count_ge(c1), count_ge(c2), count_ge(c3)
            thr = jnp.where(n3 >= cap, c3, jnp.where(n2 >= cap, c2, jnp.where(n1 >= cap, c1, thr)))
        c1 = thr | 1
        thr = jnp.where(count_ge(c1) >= cap, c1, thr)
        gt = bits > thr
        eq = bits == thr
        need = cap - jnp.sum(jnp.where(gt, 1.0, 0.0), axis=1, keepdims=True)
        sel = gt | (eq & (_prefix_excl(eq) < need))
        pos = jnp.where(sel, _prefix_excl(sel), -1.0)
        pos_ref[...] = pos
        pad = jnp.full((LANES - n_exp, t), -1.0, F32)
        pos_t_ref[...] = jnp.concatenate([pos, pad], axis=0).T

    slot = lax.broadcasted_iota(I32, (cap, t), 0).astype(F32)
    pieces = []
    for k in range(ROUTE_EXPERTS):
        ek = e * ROUTE_EXPERTS + k
        onehot = pos_ref[pl.ds(ek, 1), :] == slot
        pieces.append(jnp.where(onehot, 1.0, 0.0).astype(BF16))
        gs = jnp.sum(jnp.where(onehot, aff_ref[pl.ds(ek, 1), :], 0.0), axis=1, keepdims=True)
        gs_ref[k] = jnp.broadcast_to(gs, (cap, LANES))
    sel = jnp.concatenate(pieces, axis=0)
    d = hn_ref.shape[1]
    for n0 in range(0, d, ROUTE_COLS):
        got = jnp.dot(sel, hn_ref[:, n0:n0 + ROUTE_COLS], preferred_element_type=F32).astype(BF16)
        for k in range(ROUTE_EXPERTS):
            xe_ref[k, :, n0:n0 + ROUTE_COLS] = got[k * cap:(k + 1) * cap, :]


def _route_gather(aff_t, hn, prev, n_exp, n_seq, t, row0, slot0, slots_total):
    m, d = hn.shape
    cap = EC_CAPACITY_FACTOR * t // n_exp
    rb0, sb0 = row0 // t, slot0 // cap
    in_specs = [pl.BlockSpec((n_exp, t), lambda b, e: (0, rb0 + b)),
                pl.BlockSpec((t, d), lambda b, e: (rb0 + b, 0))]
    args = [aff_t, hn]
    aliases = {}
    if prev is not None:
        in_specs += [pl.BlockSpec(memory_space=pl.ANY), pl.BlockSpec(memory_space=pl.ANY)]
        args += list(prev)
        aliases = {2: 0, 3: 1}
    eg = ROUTE_EXPERTS
    return pl.pallas_call(
        functools.partial(_route_body, cap, prev is not None),
        grid=(n_seq, n_exp // eg),
        in_specs=in_specs,
        out_specs=[pl.BlockSpec((eg, cap, d), lambda b, e: (e, sb0 + b, 0)),
                   pl.BlockSpec((eg, cap, LANES), lambda b, e: (e, sb0 + b, 0)),
                   pl.BlockSpec((t, LANES), lambda b, e: (b, 0))],
        out_shape=[jax.ShapeDtypeStruct((n_exp, slots_total, d), BF16),
                   jax.ShapeDtypeStruct((n_exp, slots_total, LANES), F32),
                   jax.ShapeDtypeStruct((n_seq * t, LANES), F32)],
        scratch_shapes=[pltpu.VMEM((n_exp, t), F32)],
        input_output_aliases=aliases,
        compiler_params=_cparams(2, 48),
        name="route_gather",
    )(*args)


FFN_ROWS = 512


def _ffn_up_body(xe_ref, wg_ref, wu_ref, o_ref, wgb_ref, wub_ref):
    wgb_ref[...] = wg_ref[0, 0].astype(BF16)
    wub_ref[...] = wu_ref[0, 0].astype(BF16)

    def rows(r, carry):
        sl = pl.ds(pl.multiple_of(r * FFN_ROWS, FFN_ROWS), FFN_ROWS)
        x = xe_ref[0, sl, :]
        gte = jnp.dot(x, wgb_ref[...], preferred_element_type=F32)
        up = jnp.dot(x, wub_ref[...], preferred_element_type=F32)
        o_ref[0, sl, :] = (_silu(gte) * up).astype(BF16)
        return carry

    lax.fori_loop(0, xe_ref.shape[1] // FFN_ROWS, rows, 0, unroll=True)


def _ffn_down_body(h_ref, wd_ref, gs_ref, o_ref, wdb_ref):
    wdb_ref[...] = wd_ref[0, 0].astype(BF16)

    def rows(r, carry):
        sl = pl.ds(pl.multiple_of(r * FFN_ROWS, FFN_ROWS), FFN_ROWS)
        y = jnp.dot(h_ref[0, sl, :], wdb_ref[...], preferred_element_type=F32)
        o_ref[0, sl, :] = (y * gs_ref[0, sl, 0:1]).astype(BF16)
        return carry

    lax.fori_loop(0, h_ref.shape[1] // FFN_ROWS, rows, 0, unroll=True)


def _expert_ffn(xe, gs, w_gate, w_up, w_down, layer):
    n_exp, slots, d = xe.shape
    ff = w_gate.shape[3]
    tf, tn = 256, 512
    hid = pl.pallas_call(
        _ffn_up_body,
        grid=(n_exp, ff // tf),
        in_specs=[pl.BlockSpec((1, slots, d), lambda e, f: (e, 0, 0)),
                  pl.BlockSpec((1, 1, d, tf), lambda e, f: (layer, e, 0, f)),
                  pl.BlockSpec((1, 1, d, tf), lambda e, f: (layer, e, 0, f))],
        out_specs=pl.BlockSpec((1, slots, tf), lambda e, f: (e, 0, f)),
        out_shape=jax.ShapeDtypeStruct((n_exp, slots, ff), BF16),
        scratch_shapes=[pltpu.VMEM((d, tf), BF16), pltpu.VMEM((d, tf), BF16)],
        compiler_params=_cparams(2, 48),
        name="ffn_up",
    )(xe, w_gate, w_up)
    return pl.pallas_call(
        _ffn_down_body,
        grid=(n_exp, d // tn),
        in_specs=[pl.BlockSpec((1, slots, ff), lambda e, j: (e, 0, 0)),
                  pl.BlockSpec((1, 1, ff, tn), lambda e, j: (layer, e, 0, j)),
                  pl.BlockSpec((1, slots, LANES), lambda e, j: (e, 0, 0))],
        out_specs=pl.BlockSpec((1, slots, tn), lambda e, j: (e, 0, j)),
        out_shape=jax.ShapeDtypeStruct((n_exp, slots, d), BF16),
        scratch_shapes=[pltpu.VMEM((ff, tn), BF16)],
        compiler_params=_cparams(2, 48),
        name="ffn_down",
    )(hid, w_down, gs)


COMBINE_COLS = 512


def _combine_body(cap, has_alias, pos_t_ref, ye_ref, x_ref, g_ref, *rest):
    o_ref, onehot_ref = rest[-2:]
    n_exp = ye_ref.shape[0]
    t, width = onehot_ref.shape

    @pl.when(pl.program_id(1) == 0)
    def _():
        pos_b = pos_t_ref[...].astype(BF16)
        for c0 in range(0, width, COMBINE_COLS):
            col = c0 + lax.broadcasted_iota(I32, (LANES, COMBINE_COLS), 1)
            expand = jnp.where(col // cap == lax.broadcasted_iota(I32, (LANES, COMBINE_COLS), 0), 1.0, 0.0)
            spread = jnp.dot(pos_b, expand.astype(BF16), preferred_element_type=F32)
            want = ((c0 + lax.broadcasted_iota(I32, (1, COMBINE_COLS), 1)) & (cap - 1)).astype(F32)
            onehot_ref[:, c0:c0 + COMBINE_COLS] = jnp.where(spread == want, 1.0, 0.0).astype(BF16)

    acc = jnp.dot(onehot_ref[...], ye_ref[...].reshape(n_exp * cap, ye_ref.shape[2]), preferred_element_type=F32)
    o_ref[...] = x_ref[...] + g_ref[0] * acc


def _combine(pos_t, ye, x, mod, prev, n_seq, t, row0, slot0, mod_row0, mod_step):
    m, d = x.shape
    n_exp = ye.shape[0]
    cap = EC_CAPACITY_FACTOR * t // n_exp
    tn = 256
    rb0, sb0 = row0 // t, slot0 // cap
    in_specs = [pl.BlockSpec((t, LANES), lambda b, j: (b, 0)),
                pl.BlockSpec((n_exp, cap, tn), lambda b, j: (0, sb0 + b, j)),
                pl.BlockSpec((t, tn), lambda b, j: (rb0 + b, j)),
                pl.BlockSpec((1, 1, tn), lambda b, j: ((mod_row0 + mod_step * b) * N_MOD + 5, 0, j))]
    args = [pos_t, ye, x, mod]
    aliases = {}
    if prev is not None:
        in_specs.append(pl.BlockSpec(memory_space=pl.ANY))
        args.append(prev)
        aliases = {4: 0}
    return pl.pallas_call(
        functools.partial(_combine_body, cap, prev is not None),
        grid=(n_seq, d // tn),
        in_specs=in_specs,
        out_specs=pl.BlockSpec((t, tn), lambda b, j: (rb0 + b, j)),
        out_shape=jax.ShapeDtypeStruct((m, d), F32),
        scratch_shapes=[pltpu.VMEM((t, n_exp * cap), BF16)],
        input_output_aliases=aliases,
        compiler_params=_cparams(2, 48),
        name="combine",
    )(*args)


def _final_body(x_ref, g_ref, o_ref):
    x = x_ref[...]
    o_ref[...] = x * lax.rsqrt(jnp.mean(x * x, axis=-1, keepdims=True) + EPS) * g_ref[...]


def _final_norm(x, g, row0, rows):
    d = x.shape[1]
    tm = 512
    return pl.pallas_call(
        _final_body,
        grid=(rows // tm,),
        in_specs=[pl.BlockSpec((tm, d), lambda i: (row0 // tm + i, 0)),
                  pl.BlockSpec((1, d), lambda i: (0, 0))],
        out_specs=pl.BlockSpec((tm, d), lambda i: (i, 0)),
        out_shape=jax.ShapeDtypeStruct((rows, d), F32),
        compiler_params=_cparams(1, 40),
        name="final_norm",
    )(x, g.reshape(1, d))


def kernel(x_prompt, x_sample, state_ssd_fwd, state_ssd_bwd, c, c_ctx, w_ada, b_ada, norm_mix_g, norm_ffn_g, w_in, ssd_conv_w, ssd_conv_b, ssd_dt_bias, ssd_a_log, ssd_d, ssd_norm_g, conf_conv_w, conf_conv_b, conf_ln_g, conf_ln_b, sc_conv_w, w_out, router_w, w_gate, w_up, w_down, norm_final_g):
    nb, ctx_len, d = x_prompt.shape
    nd, lat_len, _ = x_sample.shape
    depth = w_in.shape[0]
    n_ctx_rows, n_lat_rows = nb * ctx_len, nd * lat_len
    heads = ssd_d.shape[1]
    hp = heads * SSD_HEAD_DIM
    gn = SSD_GROUPS * SSD_D_STATE
    conv_ch = hp + 2 * gn
    conf_w = conf_conv_w.shape[2]
    sc_w = sc_conv_w.shape[2]
    n_exp = router_w.shape[2]
    off_xbc = hp
    off_dt = off_xbc + conv_ch
    off_conf = off_dt + 2 * heads
    off_sc = off_conf + 2 * conf_w
    p_xbc, p_conf, p_sc = hp, hp + conv_ch, hp + conv_ch + 2 * conf_w

    x = jnp.concatenate([x_prompt.reshape(n_ctx_rows, d), x_sample.reshape(n_lat_rows, d)], axis=0)
    cond_rows = 16
    cond = jnp.concatenate([c_ctx[None, :], c, jnp.zeros((cond_rows - 1 - nd, d), F32)], axis=0)
    mod_all = _adaln_all(cond, w_ada, b_ada)

    st_f = state_ssd_fwd.reshape(nd, depth, hp, SSD_D_STATE)
    st_b = state_ssd_bwd.reshape(nd, depth, hp, SSD_D_STATE)
    ctx_cap = EC_CAPACITY_FACTOR * ctx_len // n_exp
    lat_cap = EC_CAPACITY_FACTOR * lat_len // n_exp
    slots_ctx = nb * ctx_cap
    slots_total = slots_ctx + nd * lat_cap
    zpad = jnp.zeros((d, LANES - heads), F32)
    lane_pad = lambda v: jnp.pad(v, ((0, 0), (0, LANES - heads))).reshape(2, 1, LANES)

    new_f, new_b = [], []
    for l in range(depth):
        mod = mod_all[l].reshape(cond_rows * N_MOD, 1, d)
        wl = w_in[l]
        w_main = jnp.concatenate([wl[:, :off_dt], wl[:, off_conf:]], axis=1).astype(BF16)
        w_dt = jnp.concatenate([wl[:, off_dt:off_dt + heads], zpad, wl[:, off_dt + heads:off_conf], zpad],
                               axis=1).astype(BF16)
        proj, dtraw = _inproj(x, norm_mix_g[l], mod, w_main, w_dt, n_ctx_rows, lat_len)

        xbc = _ssdconv(proj, p_xbc, ssd_conv_w[l], ssd_conv_b[l], n_ctx_rows, ctx_len, lat_len)
        bias, alog = lane_pad(ssd_dt_bias[l]), lane_pad(ssd_a_log[l])
        y_f, fin_f = _ssd_scan(False, xbc, dtraw, bias, alog, None, None, None, l, nb, ctx_len, 0, True)
        (y_f,) = _ssd_scan(False, xbc, dtraw, bias, alog, None, y_f, st_f, l, nd, lat_len, n_ctx_rows, False)
        gate_in = (y_f, proj, jnp.repeat(ssd_d[l], SSD_HEAD_DIM), ssd_norm_g[l])
        ymix, fin_b = _ssd_scan(True, xbc, dtraw, bias, alog, gate_in, None, None, l, nb, ctx_len, 0, True)
        (ymix,) = _ssd_scan(True, xbc, dtraw, bias, alog, gate_in, ymix, st_b, l, nd, lat_len, n_ctx_rows, False)
        new_f.append(fin_f.reshape(nb, heads, SSD_HEAD_DIM, SSD_D_STATE))
        new_b.append(fin_b.reshape(nb, heads, SSD_HEAD_DIM, SSD_D_STATE))
        u = _conformer(proj, p_conf, conf_conv_w[l], conf_conv_b[l], conf_ln_g[l], conf_ln_b[l], n_ctx_rows, ctx_len)
        v = _sconv(proj, p_sc, sc_conv_w[l], n_ctx_rows, ctx_len, lat_len)
        x = _outproj(ymix, u, v, w_out[l].astype(BF16), x, mod, n_ctx_rows, lat_len)

        rw_pad = jnp.pad(router_w[l], ((0, 0), (0, LANES - n_exp)))
        hn, aff_t = _router(x, norm_ffn_g[l], mod, rw_pad, n_exp, n_ctx_rows, lat_len)
        xe, gs, pos_c = _route_gather(aff_t, hn, None, n_exp, nb, ctx_len, 0, 0, slots_total)
        xe, gs, pos_l = _route_gather(aff_t, hn, (xe, gs), n_exp, nd, lat_len, n_ctx_rows, slots_ctx, slots_total)
        ye = _expert_ffn(xe, gs, w_gate, w_up, w_down, l)
        x2 = _combine(pos_c, ye, x, mod, None, nb, ctx_len, 0, 0, 0, 0)
        x = _combine(pos_l, ye, x, mod, x2, nd, lat_len, n_ctx_rows, slots_ctx, 1, 1)

    y_prompt = _final_norm(x, norm_final_g, 0, n_ctx_rows).reshape(nb, ctx_len, d)
    y_sample = _final_norm(x, norm_final_g, n_ctx_rows, n_lat_rows).reshape(nd, lat_len, d)
    return (y_prompt, y_sample, jnp.stack(new_f, axis=1), jnp.stack(new_b, axis=1))
```

```python
import functools

import jax
import jax.numpy as jnp
from jax import lax
from jax.experimental import pallas as pl
from jax.experimental.pallas import tpu as pltpu

F32 = jnp.float32
BF16 = jnp.bfloat16
I32 = jnp.int32
EPS = 1e-6
HIGHEST = lax.Precision.HIGHEST

GRID_W = 64
SSD_CHUNK = 128
SSD_HEAD_DIM = 64
SSD_GROUPS = 4
SSD_D_STATE = 128
N_MOD = 6
EC_CAPACITY_FACTOR = 2
LANES = 128
SUBLANES = 8
MIB = 1024 * 1024
NEG = -1e30
NORM_ROWS = 256


def _cparams(n_grid, vmem_mib):
    return pltpu.CompilerParams(dimension_semantics=("arbitrary",) * n_grid, vmem_limit_bytes=vmem_mib * MIB)


def _silu(x):
    return x * jax.nn.sigmoid(x)


def _softplus(x):
    y = jnp.exp(-jnp.abs(x))
    u = 1.0 + y
    um1 = u - 1.0
    l1p = jnp.where(um1 == 0.0, y, jnp.log(u) * (y / jnp.where(um1 == 0.0, 1.0, um1)))
    return jnp.maximum(x, 0.0) + l1p


def _mod_row(i, tm, n_ctx_rows, lat_len):
    nct = n_ctx_rows // tm
    per = lat_len // tm
    return jnp.where(i < nct, 0, 1 + (i - nct) // per)


def _adaln_body(cond_ref, w_ref, b_ref, o_ref):
    s = _silu(cond_ref[...]).astype(BF16)
    o_ref[0] = jnp.dot(s, w_ref[0].astype(BF16), preferred_element_type=F32) + b_ref[0]


def _adaln_all(cond, w_ada, b_ada):
    depth, d, n = w_ada.shape
    rows = cond.shape[0]
    tn = 1024
    return pl.pallas_call(
        _adaln_body,
        grid=(depth, n // tn),
        in_specs=[pl.BlockSpec((rows, d), lambda l, j: (0, 0)),
                  pl.BlockSpec((1, d, tn), lambda l, j: (l, 0, j)),
                  pl.BlockSpec((1, 1, tn), lambda l, j: (l, 0, j))],
        out_specs=pl.BlockSpec((1, rows, tn), lambda l, j: (l, 0, j)),
        out_shape=jax.ShapeDtypeStruct((depth, rows, n), F32),
        compiler_params=_cparams(2, 40),
        name="adaln",
    )(cond, w_ada, b_ada.reshape(depth, 1, n))


def _modnorm(x, g, sc, sh):
    r = lax.rsqrt(jnp.mean(x * x, axis=-1, keepdims=True) + EPS)
    return (x * r * g) * (1.0 + sc) + sh


def _inproj_body(x_ref, g_ref, sc_ref, sh_ref, w_ref, wdt_ref, o_ref, dt_ref, hn_ref):
    @pl.when(pl.program_id(1) == 0)
    def _():
        def rows(r, carry):
            sl = pl.ds(pl.multiple_of(r * NORM_ROWS, NORM_ROWS), NORM_ROWS)
            hb = _modnorm(x_ref[sl, :], g_ref[...], sc_ref[0], sh_ref[0]).astype(BF16)
            hn_ref[sl, :] = hb
            dt_ref[sl, :] = jnp.dot(hb, wdt_ref[...], preferred_element_type=F32)
            return carry

        lax.fori_loop(0, x_ref.shape[0] // NORM_ROWS, rows, 0)

    o_ref[...] = jnp.dot(hn_ref[...], w_ref[...], preferred_element_type=F32)


def _inproj(x, g, mod, w_main, w_dt, n_ctx_rows, lat_len):
    m, d = x.shape
    n = w_main.shape[1]
    ndt = w_dt.shape[1]
    tm, tn = 1024, 1024
    mrow = functools.partial(_mod_row, tm=tm, n_ctx_rows=n_ctx_rows, lat_len=lat_len)
    return pl.pallas_call(
        _inproj_body,
        grid=(m // tm, n // tn),
        in_specs=[pl.BlockSpec((tm, d), lambda i, j: (i, 0)),
                  pl.BlockSpec((1, d), lambda i, j: (0, 0)),
                  pl.BlockSpec((1, 1, d), lambda i, j: (mrow(i) * N_MOD + 1, 0, 0)),
                  pl.BlockSpec((1, 1, d), lambda i, j: (mrow(i) * N_MOD + 0, 0, 0)),
                  pl.BlockSpec((d, tn), lambda i, j: (0, j)),
                  pl.BlockSpec((d, ndt), lambda i, j: (0, 0))],
        out_specs=[pl.BlockSpec((tm, tn), lambda i, j: (i, j)),
                   pl.BlockSpec((tm, ndt), lambda i, j: (i, 0))],
        out_shape=[jax.ShapeDtypeStruct((m, n), F32), jax.ShapeDtypeStruct((m, ndt), F32)],
        scratch_shapes=[pltpu.VMEM((tm, d), BF16)],
        compiler_params=_cparams(2, 52),
        name="inproj",
    )(x, g.reshape(1, d), mod, mod, w_main, w_dt)


def _shift_rows(x, d, pos, seg):
    if d == 0:
        return x
    n = x.shape[0]
    y = pltpu.roll(x, (-d) % n, axis=0)
    ok = (pos >= -d) if d < 0 else (pos < seg - d)
    return jnp.where(ok, y, 0.0)


def _ssdconv_body(nct, seg_c, seg_l, x_ref, w_ref, b_ref, o_ref, buf_ref):
    i = pl.program_id(0)
    tm, ch = x_ref.shape

    def emit(r0, cols, acc):
        o_ref[r0:r0 + CONV_ROWS, cols] = _silu(acc)

    @pl.when(i < nct)
    def _():
        _seg_conv(seg_c, SUBLANES, tm, ch, lambda rows: x_ref[rows, :], w_ref, b_ref, buf_ref, emit)

    @pl.when(i >= nct)
    def _():
        _seg_conv(seg_l, SUBLANES, tm, ch, lambda rows: x_ref[rows, :], w_ref, b_ref, buf_ref, emit)


def _ssdconv(proj, col0, w, b, n_ctx_rows, ctx_len, lat_len):
    m = proj.shape[0]
    k, ch = w.shape
    tr, tc = lat_len, 256
    cb0 = col0 // tc
    return pl.pallas_call(
        functools.partial(_ssdconv_body, n_ctx_rows // tr, ctx_len, lat_len),
        grid=(m // tr, ch // tc),
        in_specs=[pl.BlockSpec((tr, tc), lambda i, j: (i, cb0 + j)),
                  pl.BlockSpec((k, tc), lambda i, j: (0, j)),
                  pl.BlockSpec((1, tc), lambda i, j: (0, j))],
        out_specs=pl.BlockSpec((tr, tc), lambda i, j: (i, j)),
        out_shape=jax.ShapeDtypeStruct((m, ch), F32),
        scratch_shapes=[pltpu.VMEM((len(_conv_shifts(k, SUBLANES)),
                                    tr // ctx_len * (ctx_len + SUBLANES) + SUBLANES, tc), F32)],
        compiler_params=_cparams(2, 40),
        name="ssdconv",
    )(proj, w, b.reshape(1, ch))


def _ssd_body(nc, backward, has_h0, want_final, *refs):
    refs = list(refs)
    xs_ref, bm_ref, cm_ref, dt_ref, bias_ref, alog_ref = refs[:6]
    refs = refs[6:]
    if has_h0:
        h0_ref = refs[0]
        refs = refs[1:]
    if backward:
        yf_ref, z_ref, dexp_ref, gn_ref = refs[:4]
        refs = refs[4:]
    if has_h0:
        refs = refs[1:]
    y_ref = refs[0]
    refs = refs[1:]
    if want_final:
        fin_ref = refs[0]
        refs = refs[1:]
    st_ref = refs[0]
    if backward:
        yb_ref = refs[1]

    c = pl.program_id(1)
    q = SSD_CHUNK
    p = SSD_HEAD_DIM
    nst = SSD_D_STATE
    heads_per_group = xs_ref.shape[1] // p // SSD_GROUPS

    @pl.when(c == 0)
    def _():
        if has_h0:
            st_ref[...] = h0_ref[0, 0].T
        else:
            st_ref[...] = jnp.zeros(st_ref.shape, F32)

    x = xs_ref[...]
    bmat = bm_ref[...]
    cmat = cm_ref[...]
    dt = _softplus(dt_ref[...] + bias_ref[0])
    a = -jnp.exp(alog_ref[0])
    dta = dt * a
    row = lax.broadcasted_iota(I32, (q, q), 0)
    col = lax.broadcasted_iota(I32, (q, q), 1)
    msk = (col >= row) if backward else (col <= row)
    d_hi = dta.astype(BF16)
    d_r1 = dta - d_hi.astype(F32)
    d_mid = d_r1.astype(BF16)
    d_lo = (d_r1 - d_mid.astype(F32)).astype(BF16)
    acum3 = jnp.dot(jnp.where(msk, 1.0, 0.0).astype(BF16), jnp.concatenate([d_hi, d_mid, d_lo], axis=1),
                    preferred_element_type=F32)
    acum = acum3[:, 0:LANES] + acum3[:, LANES:2 * LANES] + acum3[:, 2 * LANES:3 * LANES]
    total = jnp.sum(dta, axis=0, keepdims=True)
    wst = jnp.exp(total - acum) * dt
    cd = jnp.exp(total)
    acum_t = acum.T
    dt_t = dt.T

    lo = lax.broadcasted_iota(I32, (q, 2 * p), 1) < p
    lo_row = lo[0:1, :]
    gw = heads_per_group * p
    for g in range(SSD_GROUPS):
        cg = cmat[:, g * nst:(g + 1) * nst]
        bg = bmat[:, g * nst:(g + 1) * nst]
        bg_t = bg.T.astype(BF16)
        cb = lax.dot_general(cg.astype(BF16), bg.astype(BF16), (((1,), (1,)), ((), ())),
                             preferred_element_type=F32)
        stg = st_ref[:, g * gw:(g + 1) * gw]
        xdt_parts, cd_parts = [], []
        for jj in range(heads_per_group // 2):
            h0 = g * heads_per_group + 2 * jj
            sl = slice(h0 * p, (h0 + 2) * p)
            w_parts, ce_parts = [], []
            for h in (h0, h0 + 1):
                acol = jnp.broadcast_to(acum[:, h:h + 1], (q, q))
                lm = jnp.exp(jnp.where(msk, acol - acum_t[h:h + 1, :], NEG))
                w_parts.append((cb * lm * dt_t[h:h + 1, :]).astype(BF16))
                ce_parts.append((cg * jnp.exp(acol)).astype(BF16))
            lhs_parts = w_parts + ce_parts
            xs = x[:, sl]
            sts = stg[:, 2 * jj * p:(2 * jj + 2) * p]
            rhs = jnp.concatenate([jnp.where(lo, xs, 0.0).astype(BF16), jnp.where(lo, 0.0, xs).astype(BF16),
                                   jnp.where(lo, sts, 0.0).astype(BF16), jnp.where(lo, 0.0, sts).astype(BF16)],
                                  axis=0)
            y_slab = jnp.dot(jnp.concatenate(lhs_parts, axis=1), rhs, preferred_element_type=F32)
            if backward:
                yb_ref[:, sl] = y_slab
            else:
                y_ref[:, sl] = y_slab
            wpair = jnp.take_along_axis(wst, jnp.where(lo, h0, h0 + 1), axis=1)
            xdt_parts.append((xs * wpair).astype(BF16))
            cd_parts.append(jnp.where(lo_row, cd[:, h0:h0 + 1], cd[:, h0 + 1:h0 + 2]))
        new = jnp.dot(bg_t, jnp.concatenate(xdt_parts, axis=1), preferred_element_type=F32)
        st_ref[:, g * gw:(g + 1) * gw] = stg * jnp.concatenate(cd_parts, axis=1) + new

    if backward:
        v = (yf_ref[...] + yb_ref[...] + dexp_ref[...] * x) * _silu(z_ref[...])
        r = lax.rsqrt(jnp.mean(v * v, axis=-1, keepdims=True) + EPS)
        y_ref[...] = (v * r * gn_ref[...]).astype(BF16)

    if want_final:
        @pl.when(c == nc - 1)
        def _():
            fin_ref[0] = st_ref[...].T


def _ssd_scan(backward, xbc, dtraw, bias, alog, gate_in, prev, h0, layer, n_seq, seq_len, row0, want_final):
    m = xbc.shape[0]
    q = SSD_CHUNK
    nc = seq_len // q
    gn = SSD_GROUPS * SSD_D_STATE
    hp = xbc.shape[1] - 2 * gn
    blk0 = row0 // q
    d = 1 if backward else 0

    def rb(s, c):
        return blk0 + s * nc + (nc - 1 - c if backward else c)

    in_specs = [pl.BlockSpec((q, hp), lambda s, c: (rb(s, c), 0)),
                pl.BlockSpec((q, gn), lambda s, c: (rb(s, c), hp // gn)),
                pl.BlockSpec((q, gn), lambda s, c: (rb(s, c), hp // gn + 1)),
                pl.BlockSpec((q, LANES), lambda s, c: (rb(s, c), d)),
                pl.BlockSpec((1, 1, LANES), lambda s, c: (d, 0, 0)),
                pl.BlockSpec((1, 1, LANES), lambda s, c: (d, 0, 0))]
    args = [xbc, xbc, xbc, dtraw, bias, alog]
    aliases = {}
    if h0 is not None:
        in_specs.append(pl.BlockSpec((1, 1, hp, SSD_D_STATE), lambda s, c: (s, layer, 0, 0)))
        args.append(h0)
    if backward:
        y_f, proj, dexp, norm_g = gate_in
        vspec = pl.BlockSpec((1, hp), lambda s, c: (0, 0))
        in_specs += [pl.BlockSpec((q, hp), lambda s, c: (rb(s, c), 0)),
                     pl.BlockSpec((q, hp), lambda s, c: (rb(s, c), 0)), vspec, vspec]
        args += [y_f, proj, dexp.reshape(1, hp), norm_g.reshape(1, hp)]
    assert (h0 is not None) == (prev is not None)
    if prev is not None:
        in_specs.append(pl.BlockSpec(memory_space=pl.ANY))
        args.append(prev)
        aliases = {len(args) - 1: 0}
    out_specs = [pl.BlockSpec((q, hp), lambda s, c: (rb(s, c), 0))]
    out_shape = [jax.ShapeDtypeStruct((m, hp), BF16 if backward else F32)]
    if want_final:
        out_specs.append(pl.BlockSpec((1, hp, SSD_D_STATE), lambda s, c: (s, 0, 0)))
        out_shape.append(jax.ShapeDtypeStruct((n_seq, hp, SSD_D_STATE), F32))
    scratch = [pltpu.VMEM((SSD_D_STATE, hp), F32)]
    if backward:
        scratch.append(pltpu.VMEM((q, hp), F32))
    return pl.pallas_call(
        functools.partial(_ssd_body, nc, backward, h0 is not None, want_final),
        grid=(n_seq, nc),
        in_specs=in_specs,
        out_specs=out_specs,
        out_shape=out_shape,
        scratch_shapes=scratch,
        input_output_aliases=aliases,
        compiler_params=_cparams(2, 40),
        name="ssd_scan_bwd" if backward else "ssd_scan_fwd",
    )(*args)


CONV_PAD = 16
CONV_ROWS = 64
CONV_CH = 256


def _conv_shifts(k, pad):
    shifts = {(pad - (k - 1) // 2 + j) % SUBLANES for j in range(k)}
    return [0] + sorted(shifts - {0})


def _seg_conv(seg, pad, tm, ch, load_rows, w_ref, b_ref, buf_ref, emit):
    k = w_ref.shape[0]
    half = (k - 1) // 2
    nseg = tm // seg
    stride = seg + pad
    zero = jnp.zeros((pad, ch), F32)
    for j in range(nseg + 1):
        buf_ref[0, j * stride:j * stride + pad, :] = zero
    for j in range(nseg):
        buf_ref[0, j * stride + pad:(j + 1) * stride, :] = load_rows(slice(j * seg, (j + 1) * seg))
    used = nseg * stride + pad
    shifts = _conv_shifts(k, pad)
    padded = buf_ref[0, 0:used, :]
    for i, s in enumerate(shifts[1:], start=1):
        buf_ref[i, 0:used, :] = pltpu.roll(padded, used - s, axis=0)
    for r0 in range(0, tm, CONV_ROWS):
        base = (r0 // seg) * stride + pad + (r0 % seg) - half
        for c0 in range(0, ch, CONV_CH):
            cols = slice(c0, c0 + CONV_CH)
            acc = jnp.broadcast_to(b_ref[:, cols], (CONV_ROWS, CONV_CH))
            for j in range(k):
                off = base + j
                al = off - off % SUBLANES
                acc = acc + buf_ref[shifts.index(off % SUBLANES), al:al + CONV_ROWS, cols] * w_ref[j:j + 1, cols]
            emit(r0, cols, acc)


def _conf_conv(seg, val_ref, gt_ref, w_ref, b_ref, buf_ref, acc_ref):
    tm, ch = val_ref.shape

    def emit(r0, cols, acc):
        acc_ref[r0:r0 + CONV_ROWS, cols] = acc

    _seg_conv(seg, CONV_PAD, tm, ch, lambda rows: val_ref[rows, :] * jax.nn.sigmoid(gt_ref[rows, :]),
              w_ref, b_ref, buf_ref, emit)


def _conf_body(nct, seg_c, seg_l, val_ref, gt_ref, w_ref, b_ref, lg_ref, lb_ref, o_ref, buf_ref, acc_ref):
    i = pl.program_id(0)

    @pl.when(i < nct)
    def _():
        _conf_conv(seg_c, val_ref, gt_ref, w_ref, b_ref, buf_ref, acc_ref)

    @pl.when(i >= nct)
    def _():
        _conf_conv(seg_l, val_ref, gt_ref, w_ref, b_ref, buf_ref, acc_ref)

    acc = acc_ref[...]
    mu = jnp.mean(acc, axis=-1, keepdims=True)
    cen = acc - mu
    var = jnp.mean(cen * cen, axis=-1, keepdims=True)
    y = cen * lax.rsqrt(var + EPS) * lg_ref[...] + lb_ref[...]
    o_ref[...] = _silu(y).astype(BF16)


def _conformer(proj, col0, w, b, lg, lb, n_ctx_rows, ctx_len):
    m = proj.shape[0]
    k, ch = w.shape
    tm = ctx_len
    cb0 = col0 // ch
    vec = lambda v: v.reshape(1, ch)
    vspec = pl.BlockSpec((1, ch), lambda i: (0, 0))
    return pl.pallas_call(
        functools.partial(_conf_body, n_ctx_rows // tm, ctx_len, GRID_W),
        grid=(m // tm,),
        in_specs=[pl.BlockSpec((tm, ch), lambda i: (i, cb0)),
                  pl.BlockSpec((tm, ch), lambda i: (i, cb0 + 1)),
                  pl.BlockSpec((k, ch), lambda i: (0, 0)),
                  vspec, vspec, vspec],
        out_specs=pl.BlockSpec((tm, ch), lambda i: (i, 0)),
        out_shape=jax.ShapeDtypeStruct((m, ch), BF16),
        scratch_shapes=[pltpu.VMEM((SUBLANES, tm // GRID_W * (GRID_W + CONV_PAD) + CONV_PAD, ch), F32),
                        pltpu.VMEM((tm, ch), F32)],
        compiler_params=_cparams(1, 40),
        name="conformer",
    )(proj, proj, w, vec(b), vec(lg), vec(lb))


def _sconv_body(nct, seg_c, bg_ref, cg_ref, hx_ref, w_ref, o_ref):
    i = pl.program_id(0)
    v = cg_ref[...] * hx_ref[...]
    n = v.shape[0]
    it = lax.broadcasted_iota(I32, v.shape, 0)
    w0, w1, w2 = w_ref[0:1, :], w_ref[1:2, :], w_ref[2:3, :]

    @pl.when(i < nct)
    def _():
        pos = it & (seg_c - 1)
        y = w1 * v + w0 * _shift_rows(v, -1, pos, seg_c) + w2 * _shift_rows(v, 1, pos, seg_c)
        o_ref[...] = (bg_ref[...] * y).astype(BF16)

    @pl.when(i >= nct)
    def _():
        y = w1 * v + w0 * _shift_rows(v, -GRID_W, it, n) + w2 * _shift_rows(v, GRID_W, it, n)
        o_ref[...] = (bg_ref[...] * y).astype(BF16)


def _sconv(proj, col0, w, n_ctx_rows, ctx_len, lat_len):
    m = proj.shape[0]
    k, ch = w.shape
    tr, tc = lat_len, 256
    cb0 = col0 // tc
    nch = ch // tc
    return pl.pallas_call(
        functools.partial(_sconv_body, n_ctx_rows // tr, ctx_len),
        grid=(m // tr, nch),
        in_specs=[pl.BlockSpec((tr, tc), lambda i, j: (i, cb0 + j)),
                  pl.BlockSpec((tr, tc), lambda i, j: (i, cb0 + nch + j)),
                  pl.BlockSpec((tr, tc), lambda i, j: (i, cb0 + 2 * nch + j)),
                  pl.BlockSpec((k, tc), lambda i, j: (0, j))],
        out_specs=pl.BlockSpec((tr, tc), lambda i, j: (i, j)),
        out_shape=jax.ShapeDtypeStruct((m, ch), BF16),
        compiler_params=_cparams(2, 40),
        name="sconv",
    )(proj, proj, proj, w)


def _outproj_body(y_ref, u_ref, v_ref, w_ref, x_ref, g_ref, o_ref):
    ky, ku = y_ref.shape[1], u_ref.shape[1]
    acc = jnp.dot(y_ref[...], w_ref[0:ky, :], preferred_element_type=F32)
    acc = acc + jnp.dot(u_ref[...], w_ref[ky:ky + ku, :], preferred_element_type=F32)
    acc = acc + jnp.dot(v_ref[...], w_ref[ky + ku:, :], preferred_element_type=F32)
    o_ref[...] = x_ref[...] + g_ref[0] * acc


def _outproj(y, u, v, w, x, mod, n_ctx_rows, lat_len):
    m, d = x.shape
    kt = w.shape[0]
    tm, tn = 1024, 512
    mrow = functools.partial(_mod_row, tm=tm, n_ctx_rows=n_ctx_rows, lat_len=lat_len)
    return pl.pallas_call(
        _outproj_body,
        grid=(m // tm, d // tn),
        in_specs=[pl.BlockSpec((tm, y.shape[1]), lambda i, j: (i, 0)),
                  pl.BlockSpec((tm, u.shape[1]), lambda i, j: (i, 0)),
                  pl.BlockSpec((tm, v.shape[1]), lambda i, j: (i, 0)),
                  pl.BlockSpec((kt, tn), lambda i, j: (0, j)),
                  pl.BlockSpec((tm, tn), lambda i, j: (i, j)),
                  pl.BlockSpec((1, 1, tn), lambda i, j: (mrow(i) * N_MOD + 2, 0, j))],
        out_specs=pl.BlockSpec((tm, tn), lambda i, j: (i, j)),
        out_shape=jax.ShapeDtypeStruct((m, d), F32),
        compiler_params=_cparams(2, 48),
        name="outproj",
    )(y, u, v, w, x, mod)


def _router_body(n_exp, x_ref, g_ref, sc_ref, sh_ref, rw_ref, hn_ref, aff_t_ref):
    hn = _modnorm(x_ref[...], g_ref[...], sc_ref[0], sh_ref[0])
    hn_hi = hn.astype(BF16)
    hn_ref[...] = hn_hi
    hn_lo = (hn - hn_hi.astype(F32)).astype(BF16)
    w_hi = rw_ref[...].astype(BF16)
    w_lo = (rw_ref[...] - w_hi.astype(F32)).astype(BF16)
    logits = (jnp.dot(hn_hi, w_hi, preferred_element_type=F32) + jnp.dot(hn_hi, w_lo, preferred_element_type=F32)
              + jnp.dot(hn_lo, w_hi, preferred_element_type=F32))
    lane = lax.broadcasted_iota(I32, logits.shape, 1)
    logits = jnp.where(lane < n_exp, logits, NEG)
    e = jnp.exp(logits - jnp.max(logits, axis=-1, keepdims=True))
    aff = e / jnp.sum(e, axis=-1, keepdims=True)
    aff_t_ref[...] = aff.T[:n_exp, :]


def _router(x, g, mod, rw_pad, n_exp, n_ctx_rows, lat_len):
    m, d = x.shape
    tm = 256
    mrow = functools.partial(_mod_row, tm=tm, n_ctx_rows=n_ctx_rows, lat_len=lat_len)
    return pl.pallas_call(
        functools.partial(_router_body, n_exp),
        grid=(m // tm,),
        in_specs=[pl.BlockSpec((tm, d), lambda i: (i, 0)),
                  pl.BlockSpec((1, d), lambda i: (0, 0)),
                  pl.BlockSpec((1, 1, d), lambda i: (mrow(i) * N_MOD + 4, 0, 0)),
                  pl.BlockSpec((1, 1, d), lambda i: (mrow(i) * N_MOD + 3, 0, 0)),
                  pl.BlockSpec((d, LANES), lambda i: (0, 0))],
        out_specs=[pl.BlockSpec((tm, d), lambda i: (i, 0)),
                   pl.BlockSpec((n_exp, tm), lambda i: (0, i))],
        out_shape=[jax.ShapeDtypeStruct((m, d), BF16), jax.ShapeDtypeStruct((n_exp, m), F32)],
        compiler_params=_cparams(1, 40),
        name="router",
    )(x, g.reshape(1, d), mod, mod, rw_pad)


def _prefix_excl(mask):
    rows, t = mask.shape
    blk = min(t, 256)
    mb = jnp.where(mask, 1.0, 0.0)
    upper = jnp.where(lax.broadcasted_iota(I32, (blk, blk), 0) < lax.broadcasted_iota(I32, (blk, blk), 1), 1.0, 0.0)
    upper = upper.astype(BF16)
    carry = jnp.zeros((rows, 1), F32)
    outs = []
    for j in range(t // blk):
        part = mb[:, j * blk:(j + 1) * blk]
        outs.append(jnp.dot(part.astype(BF16), upper, preferred_element_type=F32) + carry)
        carry = carry + jnp.sum(part, axis=1, keepdims=True)
    return outs[0] if len(outs) == 1 else jnp.concatenate(outs, axis=1)


ROUTE_ROWS = 1024
ROUTE_COLS = 1024


def _route_body(cap, n_group, *refs):
    aff_ref, hn_ref = refs[:2]
    xe_ref, gs_ref, pos_t_ref, pos_ref = refs[-4:]
    e = pl.program_id(1)
    n_exp, t = aff_ref.shape

    @pl.when(e == 0)
    def _():
        a = aff_ref[...]
        bits = lax.bitcast_convert_type(a, I32)
        thr = jnp.zeros((n_exp, 1), I32)
        count_ge = lambda cand: jnp.sum(jnp.where(bits >= cand, 1.0, 0.0), axis=1, keepdims=True)
        for bit in range(29, 0, -2):
            c1, c2, c3 = thr | (1 << bit), thr | (2 << bit), thr | (3 << bit)
            n1, n2, n3 = count_ge(c1), count_ge(c2), count_ge(c3)
            thr = jnp.where(n3 >= cap, c3, jnp.where(n2 >= cap, c2, jnp.where(n1 >= cap, c1, thr)))
        c1 = thr | 1
        thr = jnp.where(count_ge(c1) >= cap, c1, thr)
        gt = bits > thr
        eq = bits == thr
        need = cap - jnp.sum(jnp.where(gt, 1.0, 0.0), axis=1, keepdims=True)
        sel = gt | (eq & (_prefix_excl(eq) < need))
        pos = jnp.where(sel, _prefix_excl(sel), -1.0)
        pos_ref[...] = pos
        pad = jnp.full((LANES - n_exp, t), -1.0, F32)
        pos_t_ref[...] = jnp.concatenate([pos, pad], axis=0).T

    slot = lax.broadcasted_iota(I32, (cap, t), 0).astype(F32)
    pieces = []
    for k in range(n_group):
        ek = e * n_group + k
        onehot = pos_ref[pl.ds(ek, 1), :] == slot
        pieces.append(jnp.where(onehot, 1.0, 0.0).astype(BF16))
        gs = jnp.sum(jnp.where(onehot, aff_ref[pl.ds(ek, 1), :], 0.0), axis=1, keepdims=True)
        gs_ref[k] = jnp.broadcast_to(gs, (cap, LANES))
    sel = jnp.concatenate(pieces, axis=0)
    d = hn_ref.shape[1]
    for n0 in range(0, d, ROUTE_COLS):
        got = jnp.dot(sel, hn_ref[:, n0:n0 + ROUTE_COLS], preferred_element_type=F32).astype(BF16)
        for k in range(n_group):
            xe_ref[k, :, n0:n0 + ROUTE_COLS] = got[k * cap:(k + 1) * cap, :]


def _route_gather(aff_t, hn, prev, n_exp, n_seq, t, row0, slot0, slots_total):
    m, d = hn.shape
    cap = EC_CAPACITY_FACTOR * t // n_exp
    rb0, sb0 = row0 // t, slot0 // cap
    in_specs = [pl.BlockSpec((n_exp, t), lambda b, e: (0, rb0 + b)),
                pl.BlockSpec((t, d), lambda b, e: (rb0 + b, 0))]
    args = [aff_t, hn]
    aliases = {}
    if prev is not None:
        in_specs += [pl.BlockSpec(memory_space=pl.ANY), pl.BlockSpec(memory_space=pl.ANY)]
        args += list(prev)
        aliases = {2: 0, 3: 1}
    eg = min(n_exp, ROUTE_ROWS // cap)
    return pl.pallas_call(
        functools.partial(_route_body, cap, eg),
        grid=(n_seq, n_exp // eg),
        in_specs=in_specs,
        out_specs=[pl.BlockSpec((eg, cap, d), lambda b, e: (e, sb0 + b, 0)),
                   pl.BlockSpec((eg, cap, LANES), lambda b, e: (e, sb0 + b, 0)),
                   pl.BlockSpec((t, LANES), lambda b, e: (b, 0))],
        out_shape=[jax.ShapeDtypeStruct((n_exp, slots_total, d), BF16),
                   jax.ShapeDtypeStruct((n_exp, slots_total, LANES), F32),
                   jax.ShapeDtypeStruct((n_seq * t, LANES), F32)],
        scratch_shapes=[pltpu.VMEM((n_exp, t), F32)],
        input_output_aliases=aliases,
        compiler_params=_cparams(2, 48),
        name="route_gather",
    )(*args)


FFN_ROWS = 512


def _ffn_up_body(xe_ref, wg_ref, wu_ref, o_ref, wgb_ref, wub_ref):
    wgb_ref[...] = wg_ref[0, 0].astype(BF16)
    wub_ref[...] = wu_ref[0, 0].astype(BF16)

    def rows(r, carry):
        sl = pl.ds(pl.multiple_of(r * FFN_ROWS, FFN_ROWS), FFN_ROWS)
        x = xe_ref[0, sl, :]
        gte = jnp.dot(x, wgb_ref[...], preferred_element_type=F32)
        up = jnp.dot(x, wub_ref[...], preferred_element_type=F32)
        o_ref[0, sl, :] = (_silu(gte) * up).astype(BF16)
        return carry

    lax.fori_loop(0, xe_ref.shape[1] // FFN_ROWS, rows, 0, unroll=True)


def _ffn_down_body(h_ref, wd_ref, gs_ref, o_ref, wdb_ref):
    wdb_ref[...] = wd_ref[0, 0].astype(BF16)

    def rows(r, carry):
        sl = pl.ds(pl.multiple_of(r * FFN_ROWS, FFN_ROWS), FFN_ROWS)
        y = jnp.dot(h_ref[0, sl, :], wdb_ref[...], preferred_element_type=F32)
        o_ref[0, sl, :] = (y * gs_ref[0, sl, 0:1]).astype(BF16)
        return carry

    lax.fori_loop(0, h_ref.shape[1] // FFN_ROWS, rows, 0, unroll=True)


def _expert_ffn(xe, gs, w_gate, w_up, w_down, layer):
    n_exp, slots, d = xe.shape
    ff = w_gate.shape[3]
    tf, tn = 256, 512
    hid = pl.pallas_call(
        _ffn_up_body,
        grid=(n_exp, ff // tf),
        in_specs=[pl.BlockSpec((1, slots, d), lambda e, f: (e, 0, 0)),
                  pl.BlockSpec((1, 1, d, tf), lambda e, f: (layer, e, 0, f)),
                  pl.BlockSpec((1, 1, d, tf), lambda e, f: (layer, e, 0, f))],
        out_specs=pl.BlockSpec((1, slots, tf), lambda e, f: (e, 0, f)),
        out_shape=jax.ShapeDtypeStruct((n_exp, slots, ff), BF16),
        scratch_shapes=[pltpu.VMEM((d, tf), BF16), pltpu.VMEM((d, tf), BF16)],
        compiler_params=_cparams(2, 48),
        name="ffn_up",
    )(xe, w_gate, w_up)
    return pl.pallas_call(
        _ffn_down_body,
        grid=(n_exp, d // tn),
        in_specs=[pl.BlockSpec((1, slots, ff), lambda e, j: (e, 0, 0)),
                  pl.BlockSpec((1, 1, ff, tn), lambda e, j: (layer, e, 0, j)),
                  pl.BlockSpec((1, slots, LANES), lambda e, j: (e, 0, 0))],
        out_specs=pl.BlockSpec((1, slots, tn), lambda e, j: (e, 0, j)),
        out_shape=jax.ShapeDtypeStruct((n_exp, slots, d), BF16),
        scratch_shapes=[pltpu.VMEM((ff, tn), BF16)],
        compiler_params=_cparams(2, 48),
        name="ffn_down",
    )(hid, w_down, gs)


COMBINE_COLS = 512
COMBINE_BLOCK_ELEMS = 512 * 1024


def _combine_body(cap, has_alias, pos_t_ref, ye_ref, x_ref, g_ref, *rest):
    o_ref, onehot_ref = rest[-2:]
    n_exp = ye_ref.shape[0]
    t, width = onehot_ref.shape

    @pl.when(pl.program_id(1) == 0)
    def _():
        pos_b = pos_t_ref[...].astype(BF16)
        cols = min(COMBINE_COLS, width)
        for c0 in range(0, width, cols):
            col = c0 + lax.broadcasted_iota(I32, (LANES, cols), 1)
            expand = jnp.where(col // cap == lax.broadcasted_iota(I32, (LANES, cols), 0), 1.0, 0.0)
            spread = jnp.dot(pos_b, expand.astype(BF16), preferred_element_type=F32)
            want = ((c0 + lax.broadcasted_iota(I32, (1, cols), 1)) & (cap - 1)).astype(F32)
            onehot_ref[:, c0:c0 + cols] = jnp.where(spread == want, 1.0, 0.0).astype(BF16)

    acc = jnp.dot(onehot_ref[...], ye_ref[...].reshape(n_exp * cap, ye_ref.shape[2]), preferred_element_type=F32)
    o_ref[...] = x_ref[...] + g_ref[0] * acc


def _combine(pos_t, ye, x, mod, prev, n_seq, t, row0, slot0, mod_row0, mod_step):
    m, d = x.shape
    n_exp = ye.shape[0]
    cap = EC_CAPACITY_FACTOR * t // n_exp
    tn = max(256, min(d, COMBINE_BLOCK_ELEMS // t))
    rb0, sb0 = row0 // t, slot0 // cap
    in_specs = [pl.BlockSpec((t, LANES), lambda b, j: (b, 0)),
                pl.BlockSpec((n_exp, cap, tn), lambda b, j: (0, sb0 + b, j)),
                pl.BlockSpec((t, tn), lambda b, j: (rb0 + b, j)),
                pl.BlockSpec((1, 1, tn), lambda b, j: ((mod_row0 + mod_step * b) * N_MOD + 5, 0, j))]
    args = [pos_t, ye, x, mod]
    aliases = {}
    if prev is not None:
        in_specs.append(pl.BlockSpec(memory_space=pl.ANY))
        args.append(prev)
        aliases = {4: 0}
    return pl.pallas_call(
        functools.partial(_combine_body, cap, prev is not None),
        grid=(n_seq, d // tn),
        in_specs=in_specs,
        out_specs=pl.BlockSpec((t, tn), lambda b, j: (rb0 + b, j)),
        out_shape=jax.ShapeDtypeStruct((m, d), F32),
        scratch_shapes=[pltpu.VMEM((t, n_exp * cap), BF16)],
        input_output_aliases=aliases,
        compiler_params=_cparams(2, 48),
        name="combine",
    )(*args)


def _final_body(x_ref, g_ref, o_ref):
    x = x_ref[...]
    o_ref[...] = x * lax.rsqrt(jnp.mean(x * x, axis=-1, keepdims=True) + EPS) * g_ref[...]


def _final_norm(x, g, row0, rows):
    d = x.shape[1]
    tm = 512
    return pl.pallas_call(
        _final_body,
        grid=(rows // tm,),
        in_specs=[pl.BlockSpec((tm, d), lambda i: (row0 // tm + i, 0)),
                  pl.BlockSpec((1, d), lambda i: (0, 0))],
        out_specs=pl.BlockSpec((tm, d), lambda i: (i, 0)),
        out_shape=jax.ShapeDtypeStruct((rows, d), F32),
        compiler_params=_cparams(1, 40),
        name="final_norm",
    )(x, g.reshape(1, d))


def kernel(x_prompt, x_sample, state_ssd_fwd, state_ssd_bwd, c, c_ctx, w_ada, b_ada, norm_mix_g, norm_ffn_g, w_in, ssd_conv_w, ssd_conv_b, ssd_dt_bias, ssd_a_log, ssd_d, ssd_norm_g, conf_conv_w, conf_conv_b, conf_ln_g, conf_ln_b, sc_conv_w, w_out, router_w, w_gate, w_up, w_down, norm_final_g):
    nb, ctx_len, d = x_prompt.shape
    nd, lat_len, _ = x_sample.shape
    depth = w_in.shape[0]
    n_ctx_rows, n_lat_rows = nb * ctx_len, nd * lat_len
    heads = ssd_d.shape[1]
    hp = heads * SSD_HEAD_DIM
    gn = SSD_GROUPS * SSD_D_STATE
    conv_ch = hp + 2 * gn
    conf_w = conf_conv_w.shape[2]
    sc_w = sc_conv_w.shape[2]
    n_exp = router_w.shape[2]
    off_xbc = hp
    off_dt = off_xbc + conv_ch
    off_conf = off_dt + 2 * heads
    off_sc = off_conf + 2 * conf_w
    p_xbc, p_conf, p_sc = hp, hp + conv_ch, hp + conv_ch + 2 * conf_w

    x = jnp.concatenate([x_prompt.reshape(n_ctx_rows, d), x_sample.reshape(n_lat_rows, d)], axis=0)
    cond_rows = 16
    cond = jnp.concatenate([c_ctx[None, :], c, jnp.zeros((cond_rows - 1 - nd, d), F32)], axis=0)
    mod_all = _adaln_all(cond, w_ada, b_ada)

    st_f = state_ssd_fwd.reshape(nd, depth, hp, SSD_D_STATE)
    st_b = state_ssd_bwd.reshape(nd, depth, hp, SSD_D_STATE)
    ctx_cap = EC_CAPACITY_FACTOR * ctx_len // n_exp
    lat_cap = EC_CAPACITY_FACTOR * lat_len // n_exp
    slots_ctx = nb * ctx_cap
    slots_total = slots_ctx + nd * lat_cap
    zpad = jnp.zeros((d, LANES - heads), F32)
    lane_pad = lambda v: jnp.pad(v, ((0, 0), (0, LANES - heads))).reshape(2, 1, LANES)

    new_f, new_b = [], []
    for l in range(depth):
        mod = mod_all[l].reshape(cond_rows * N_MOD, 1, d)
        wl = w_in[l]
        w_main = jnp.concatenate([wl[:, :off_dt], wl[:, off_conf:]], axis=1).astype(BF16)
        w_dt = jnp.concatenate([wl[:, off_dt:off_dt + heads], zpad, wl[:, off_dt + heads:off_conf], zpad],
                               axis=1).astype(BF16)
        proj, dtraw = _inproj(x, norm_mix_g[l], mod, w_main, w_dt, n_ctx_rows, lat_len)

        xbc = _ssdconv(proj, p_xbc, ssd_conv_w[l], ssd_conv_b[l], n_ctx_rows, ctx_len, lat_len)
        bias, alog = lane_pad(ssd_dt_bias[l]), lane_pad(ssd_a_log[l])
        y_f, fin_f = _ssd_scan(False, xbc, dtraw, bias, alog, None, None, None, l, nb, ctx_len, 0, True)
        (y_f,) = _ssd_scan(False, xbc, dtraw, bias, alog, None, y_f, st_f, l, nd, lat_len, n_ctx_rows, False)
        gate_in = (y_f, proj, jnp.repeat(ssd_d[l], SSD_HEAD_DIM), ssd_norm_g[l])
        ymix, fin_b = _ssd_scan(True, xbc, dtraw, bias, alog, gate_in, None, None, l, nb, ctx_len, 0, True)
        (ymix,) = _ssd_scan(True, xbc, dtraw, bias, alog, gate_in, ymix, st_b, l, nd, lat_len, n_ctx_rows, False)
        new_f.append(fin_f.reshape(nb, heads, SSD_HEAD_DIM, SSD_D_STATE))
        new_b.append(fin_b.reshape(nb, heads, SSD_HEAD_DIM, SSD_D_STATE))
        u = _conformer(proj, p_conf, conf_conv_w[l], conf_conv_b[l], conf_ln_g[l], conf_ln_b[l], n_ctx_rows, ctx_len)
        v = _sconv(proj, p_sc, sc_conv_w[l], n_ctx_rows, ctx_len, lat_len)
        x = _outproj(ymix, u, v, w_out[l].astype(BF16), x, mod, n_ctx_rows, lat_len)

        rw_pad = jnp.pad(router_w[l], ((0, 0), (0, LANES - n_exp)))
        hn, aff_t = _router(x, norm_ffn_g[l], mod, rw_pad, n_exp, n_ctx_rows, lat_len)
        xe, gs, pos_c = _route_gather(aff_t, hn, None, n_exp, nb, ctx_len, 0, 0, slots_total)
        xe, gs, pos_l = _route_gather(aff_t, hn, (xe, gs), n_exp, nd, lat_len, n_ctx_rows, slots_ctx, slots_total)
        ye = _expert_ffn(xe, gs, w_gate, w_up, w_down, l)
        x2 = _combine(pos_c, ye, x, mod, None, nb, ctx_len, 0, 0, 0, 0)
        x = _combine(pos_l, ye, x, mod, x2, nd, lat_len, n_ctx_rows, slots_ctx, 1, 1)

    y_prompt = _final_norm(x, norm_final_g, 0, n_ctx_rows).reshape(nb, ctx_len, d)
    y_sample = _final_norm(x, norm_final_g, n_ctx_rows, n_lat_rows).reshape(nd, lat_len, d)
    return (y_prompt, y_sample, jnp.stack(new_f, axis=1), jnp.stack(new_b, axis=1))
```

```python
import functools

import jax
import jax.numpy as jnp
from jax import lax
from jax.experimental import pallas as pl
from jax.experimental.pallas import tpu as pltpu

F32 = jnp.float32
BF16 = jnp.bfloat16
I32 = jnp.int32
EPS = 1e-6
HIGHEST = lax.Precision.HIGHEST

GRID_W = 64
SSD_CHUNK = 128
SSD_HEAD_DIM = 64
SSD_GROUPS = 4
SSD_D_STATE = 128
N_MOD = 6
EC_CAPACITY_FACTOR = 2
LANES = 128
SUBLANES = 8
MIB = 1024 * 1024
NEG = -1e30
NORM_ROWS = 256


def _cparams(n_grid, vmem_mib):
    return pltpu.CompilerParams(dimension_semantics=("arbitrary",) * n_grid, vmem_limit_bytes=vmem_mib * MIB)


def _silu(x):
    return x * jax.nn.sigmoid(x)


def _softplus(x):
    y = jnp.exp(-jnp.abs(x))
    u = 1.0 + y
    um1 = u - 1.0
    l1p = jnp.where(um1 == 0.0, y, jnp.log(u) * (y / jnp.where(um1 == 0.0, 1.0, um1)))
    return jnp.maximum(x, 0.0) + l1p


def _mod_row(i, tm, n_ctx_rows, lat_len):
    nct = n_ctx_rows // tm
    per = lat_len // tm
    return jnp.where(i < nct, 0, 1 + (i - nct) // per)


def _adaln_body(cond_ref, w_ref, b_ref, o_ref):
    s = _silu(cond_ref[...]).astype(BF16)
    o_ref[0] = jnp.dot(s, w_ref[0].astype(BF16), preferred_element_type=F32) + b_ref[0]


def _adaln_all(cond, w_ada, b_ada):
    depth, d, n = w_ada.shape
    rows = cond.shape[0]
    tn = 1024
    return pl.pallas_call(
        _adaln_body,
        grid=(depth, n // tn),
        in_specs=[pl.BlockSpec((rows, d), lambda l, j: (0, 0)),
                  pl.BlockSpec((1, d, tn), lambda l, j: (l, 0, j)),
                  pl.BlockSpec((1, 1, tn), lambda l, j: (l, 0, j))],
        out_specs=pl.BlockSpec((1, rows, tn), lambda l, j: (l, 0, j)),
        out_shape=jax.ShapeDtypeStruct((depth, rows, n), F32),
        compiler_params=_cparams(2, 40),
        name="adaln",
    )(cond, w_ada, b_ada.reshape(depth, 1, n))


def _modnorm(x, g, sc, sh):
    r = lax.rsqrt(jnp.mean(x * x, axis=-1, keepdims=True) + EPS)
    return (x * r * g) * (1.0 + sc) + sh


def _inproj_body(x_ref, g_ref, sc_ref, sh_ref, w_ref, wdt_ref, o_ref, dt_ref, hn_ref):
    @pl.when(pl.program_id(1) == 0)
    def _():
        def rows(r, carry):
            sl = pl.ds(pl.multiple_of(r * NORM_ROWS, NORM_ROWS), NORM_ROWS)
            hb = _modnorm(x_ref[sl, :], g_ref[...], sc_ref[0], sh_ref[0]).astype(BF16)
            hn_ref[sl, :] = hb
            dt_ref[sl, :] = jnp.dot(hb, wdt_ref[...], preferred_element_type=F32)
            return carry

        lax.fori_loop(0, x_ref.shape[0] // NORM_ROWS, rows, 0)

    o_ref[...] = jnp.dot(hn_ref[...], w_ref[...], preferred_element_type=F32)


def _inproj(x, g, mod, w_main, w_dt, n_ctx_rows, lat_len):
    m, d = x.shape
    n = w_main.shape[1]
    ndt = w_dt.shape[1]
    tm, tn = 1024, 1024
    mrow = functools.partial(_mod_row, tm=tm, n_ctx_rows=n_ctx_rows, lat_len=lat_len)
    return pl.pallas_call(
        _inproj_body,
        grid=(m // tm, n // tn),
        in_specs=[pl.BlockSpec((tm, d), lambda i, j: (i, 0)),
                  pl.BlockSpec((1, d), lambda i, j: (0, 0)),
                  pl.BlockSpec((1, 1, d), lambda i, j: (mrow(i) * N_MOD + 1, 0, 0)),
                  pl.BlockSpec((1, 1, d), lambda i, j: (mrow(i) * N_MOD + 0, 0, 0)),
                  pl.BlockSpec((d, tn), lambda i, j: (0, j)),
                  pl.BlockSpec((d, ndt), lambda i, j: (0, 0))],
        out_specs=[pl.BlockSpec((tm, tn), lambda i, j: (i, j)),
                   pl.BlockSpec((tm, ndt), lambda i, j: (i, 0))],
        out_shape=[jax.ShapeDtypeStruct((m, n), F32), jax.ShapeDtypeStruct((m, ndt), F32)],
        scratch_shapes=[pltpu.VMEM((tm, d), BF16)],
        compiler_params=_cparams(2, 52),
        name="inproj",
    )(x, g.reshape(1, d), mod, mod, w_main, w_dt)


def _shift_rows(x, d, pos, seg):
    if d == 0:
        return x
    n = x.shape[0]
    y = pltpu.roll(x, (-d) % n, axis=0)
    ok = (pos >= -d) if d < 0 else (pos < seg - d)
    return jnp.where(ok, y, 0.0)


def _ssdconv_body(nct, seg_c, seg_l, x_ref, w_ref, b_ref, o_ref, buf_ref):
    i = pl.program_id(0)
    tm, ch = x_ref.shape

    def emit(r0, cols, acc):
        o_ref[r0:r0 + CONV_ROWS, cols] = _silu(acc)

    @pl.when(i < nct)
    def _():
        _seg_conv(seg_c, SUBLANES, tm, ch, lambda rows: x_ref[rows, :], w_ref, b_ref, buf_ref, emit)

    @pl.when(i >= nct)
    def _():
        _seg_conv(seg_l, SUBLANES, tm, ch, lambda rows: x_ref[rows, :], w_ref, b_ref, buf_ref, emit)


def _ssdconv(proj, col0, w, b, n_ctx_rows, ctx_len, lat_len):
    m = proj.shape[0]
    k, ch = w.shape
    tr, tc = lat_len, 256
    cb0 = col0 // tc
    return pl.pallas_call(
        functools.partial(_ssdconv_body, n_ctx_rows // tr, ctx_len, lat_len),
        grid=(m // tr, ch // tc),
        in_specs=[pl.BlockSpec((tr, tc), lambda i, j: (i, cb0 + j)),
                  pl.BlockSpec((k, tc), lambda i, j: (0, j)),
                  pl.BlockSpec((1, tc), lambda i, j: (0, j))],
        out_specs=pl.BlockSpec((tr, tc), lambda i, j: (i, j)),
        out_shape=jax.ShapeDtypeStruct((m, ch), F32),
        scratch_shapes=[pltpu.VMEM((len(_conv_shifts(k, SUBLANES)),
                                    tr // ctx_len * (ctx_len + SUBLANES) + SUBLANES, tc), F32)],
        compiler_params=_cparams(2, 40),
        name="ssdconv",
    )(proj, w, b.reshape(1, ch))


def _ssd_body(nc, backward, has_h0, want_final, *refs):
    refs = list(refs)
    xs_ref, bm_ref, cm_ref, dt_ref, bias_ref, alog_ref = refs[:6]
    refs = refs[6:]
    if has_h0:
        h0_ref = refs[0]
        refs = refs[1:]
    if backward:
        yf_ref, z_ref, dexp_ref, gn_ref = refs[:4]
        refs = refs[5:]
    y_ref = refs[0]
    refs = refs[1:]
    if want_final:
        fin_ref = refs[0]
        refs = refs[1:]
    st_ref = refs[0]
    if backward:
        yb_ref = refs[1]

    c = pl.program_id(1)
    q = SSD_CHUNK
    p = SSD_HEAD_DIM
    nst = SSD_D_STATE
    heads_per_group = xs_ref.shape[1] // p // SSD_GROUPS

    @pl.when(c == 0)
    def _():
        if has_h0:
            st_ref[...] = h0_ref[0, 0].T
        else:
            st_ref[...] = jnp.zeros(st_ref.shape, F32)

    x = xs_ref[...]
    bmat = bm_ref[...]
    cmat = cm_ref[...]
    dt = _softplus(dt_ref[...] + bias_ref[0])
    a = -jnp.exp(alog_ref[0])
    dta = dt * a
    row = lax.broadcasted_iota(I32, (q, q), 0)
    col = lax.broadcasted_iota(I32, (q, q), 1)
    msk = (col >= row) if backward else (col <= row)
    d_hi = dta.astype(BF16)
    d_r1 = dta - d_hi.astype(F32)
    d_mid = d_r1.astype(BF16)
    d_lo = (d_r1 - d_mid.astype(F32)).astype(BF16)
    acum3 = jnp.dot(jnp.where(msk, 1.0, 0.0).astype(BF16), jnp.concatenate([d_hi, d_mid, d_lo], axis=1),
                    preferred_element_type=F32)
    acum = acum3[:, 0:LANES] + acum3[:, LANES:2 * LANES] + acum3[:, 2 * LANES:3 * LANES]
    total = jnp.sum(dta, axis=0, keepdims=True)
    wst = jnp.exp(total - acum) * dt
    cd = jnp.exp(total)
    acum_t = acum.T
    dt_t = dt.T

    lo = lax.broadcasted_iota(I32, (q, 2 * p), 1) < p
    lo_row = lo[0:1, :]
    gw = heads_per_group * p
    for g in range(SSD_GROUPS):
        cg = cmat[:, g * nst:(g + 1) * nst]
        bg = bmat[:, g * nst:(g + 1) * nst]
        bg_t = bg.T.astype(BF16)
        cb = lax.dot_general(cg.astype(BF16), bg.astype(BF16), (((1,), (1,)), ((), ())),
                             preferred_element_type=F32)
        stg = st_ref[:, g * gw:(g + 1) * gw]
        xdt_parts, cd_parts = [], []
        for jj in range(heads_per_group // 2):
            h0 = g * heads_per_group + 2 * jj
            sl = slice(h0 * p, (h0 + 2) * p)
            w_parts, ce_parts = [], []
            for h in (h0, h0 + 1):
                acol = jnp.broadcast_to(acum[:, h:h + 1], (q, q))
                lm = jnp.exp(jnp.where(msk, acol - acum_t[h:h + 1, :], NEG))
                w_parts.append((cb * lm * dt_t[h:h + 1, :]).astype(BF16))
                ce_parts.append((cg * jnp.exp(acol)).astype(BF16))
            lhs_parts = w_parts + ce_parts
            xs = x[:, sl]
            sts = stg[:, 2 * jj * p:(2 * jj + 2) * p]
            rhs = jnp.concatenate([jnp.where(lo, xs, 0.0).astype(BF16), jnp.where(lo, 0.0, xs).astype(BF16),
                                   jnp.where(lo, sts, 0.0).astype(BF16), jnp.where(lo, 0.0, sts).astype(BF16)],
                                  axis=0)
            y_slab = jnp.dot(jnp.concatenate(lhs_parts, axis=1), rhs, preferred_element_type=F32)
            if backward:
                yb_ref[:, sl] = y_slab
            else:
                y_ref[:, sl] = y_slab
            wpair = jnp.take_along_axis(wst, jnp.where(lo, h0, h0 + 1), axis=1)
            xdt_parts.append((xs * wpair).astype(BF16))
            cd_parts.append(jnp.where(lo_row, cd[:, h0:h0 + 1], cd[:, h0 + 1:h0 + 2]))
        new = jnp.dot(bg_t, jnp.concatenate(xdt_parts, axis=1), preferred_element_type=F32)
        st_ref[:, g * gw:(g + 1) * gw] = stg * jnp.concatenate(cd_parts, axis=1) + new

    if backward:
        v = (yf_ref[...] + yb_ref[...] + dexp_ref[...] * x) * _silu(z_ref[...])
        r = lax.rsqrt(jnp.mean(v * v, axis=-1, keepdims=True) + EPS)
        y_ref[...] = (v * r * gn_ref[...]).astype(BF16)

    if want_final:
        @pl.when(c == nc - 1)
        def _():
            fin_ref[0] = st_ref[...].T


def _ssd_scan(backward, xbc, dtraw, bias, alog, gate_in, into, h0, layer, n_seq, seq_len, row0, want_final):
    m = xbc.shape[0]
    q = SSD_CHUNK
    nc = seq_len // q
    gn = SSD_GROUPS * SSD_D_STATE
    hp = xbc.shape[1] - 2 * gn
    blk0 = row0 // q
    d = 1 if backward else 0

    def rel(s, c):
        return s * nc + (nc - 1 - c if backward else c)

    def rb(s, c):
        return blk0 + rel(s, c)

    in_specs = [pl.BlockSpec((q, hp), lambda s, c: (rb(s, c), 0)),
                pl.BlockSpec((q, gn), lambda s, c: (rb(s, c), hp // gn)),
                pl.BlockSpec((q, gn), lambda s, c: (rb(s, c), hp // gn + 1)),
                pl.BlockSpec((q, LANES), lambda s, c: (rb(s, c), d)),
                pl.BlockSpec((1, 1, LANES), lambda s, c: (d, 0, 0)),
                pl.BlockSpec((1, 1, LANES), lambda s, c: (d, 0, 0))]
    args = [xbc, xbc, xbc, dtraw, bias, alog]
    aliases = {}
    if h0 is not None:
        in_specs.append(pl.BlockSpec((1, 1, hp, SSD_D_STATE), lambda s, c: (s, layer, 0, 0)))
        args.append(h0)
    if backward:
        y_f, proj, dexp, norm_g = gate_in
        vspec = pl.BlockSpec((1, hp), lambda s, c: (0, 0))
        in_specs += [pl.BlockSpec((q, hp), lambda s, c: (rel(s, c), 0)),
                     pl.BlockSpec((q, hp), lambda s, c: (rb(s, c), 0)), vspec, vspec,
                     pl.BlockSpec(memory_space=pl.ANY)]
        args += [y_f, proj, dexp.reshape(1, hp), norm_g.reshape(1, hp), into]
        aliases = {len(args) - 1: 0}
        out_specs = [pl.BlockSpec((q, hp), lambda s, c: (rb(s, c), 0))]
        out_shape = [jax.ShapeDtypeStruct((m, hp), BF16)]
    else:
        out_specs = [pl.BlockSpec((q, hp), lambda s, c: (rel(s, c), 0))]
        out_shape = [jax.ShapeDtypeStruct((n_seq * seq_len, hp), F32)]
    if want_final:
        out_specs.append(pl.BlockSpec((1, hp, SSD_D_STATE), lambda s, c: (s, 0, 0)))
        out_shape.append(jax.ShapeDtypeStruct((n_seq, hp, SSD_D_STATE), F32))
    scratch = [pltpu.VMEM((SSD_D_STATE, hp), F32)]
    if backward:
        scratch.append(pltpu.VMEM((q, hp), F32))
    return pl.pallas_call(
        functools.partial(_ssd_body, nc, backward, h0 is not None, want_final),
        grid=(n_seq, nc),
        in_specs=in_specs,
        out_specs=out_specs,
        out_shape=out_shape,
        scratch_shapes=scratch,
        input_output_aliases=aliases,
        compiler_params=_cparams(2, 40),
        name="ssd_scan_bwd" if backward else "ssd_scan_fwd",
    )(*args)


CONV_PAD = 16
CONV_ROWS = 64
CONV_CH = 256


def _conv_shifts(k, pad):
    shifts = {(pad - (k - 1) // 2 + j) % SUBLANES for j in range(k)}
    return [0] + sorted(shifts - {0})


def _seg_conv(seg, pad, tm, ch, load_rows, w_ref, b_ref, buf_ref, emit):
    k = w_ref.shape[0]
    half = (k - 1) // 2
    nseg = tm // seg
    stride = seg + pad
    zero = jnp.zeros((pad, ch), F32)
    for j in range(nseg + 1):
        buf_ref[0, j * stride:j * stride + pad, :] = zero
    for j in range(nseg):
        buf_ref[0, j * stride + pad:(j + 1) * stride, :] = load_rows(slice(j * seg, (j + 1) * seg))
    used = nseg * stride + pad
    shifts = _conv_shifts(k, pad)
    padded = buf_ref[0, 0:used, :]
    for i, s in enumerate(shifts[1:], start=1):
        buf_ref[i, 0:used, :] = pltpu.roll(padded, used - s, axis=0)
    for r0 in range(0, tm, CONV_ROWS):
        base = (r0 // seg) * stride + pad + (r0 % seg) - half
        for c0 in range(0, ch, CONV_CH):
            cols = slice(c0, c0 + CONV_CH)
            acc = jnp.broadcast_to(b_ref[:, cols], (CONV_ROWS, CONV_CH))
            for j in range(k):
                off = base + j
                al = off - off % SUBLANES
                acc = acc + buf_ref[shifts.index(off % SUBLANES), al:al + CONV_ROWS, cols] * w_ref[j:j + 1, cols]
            emit(r0, cols, acc)


def _conf_conv(seg, val_ref, gt_ref, w_ref, b_ref, buf_ref, acc_ref):
    tm, ch = val_ref.shape

    def emit(r0, cols, acc):
        acc_ref[r0:r0 + CONV_ROWS, cols] = acc

    _seg_conv(seg, CONV_PAD, tm, ch, lambda rows: val_ref[rows, :] * jax.nn.sigmoid(gt_ref[rows, :]),
              w_ref, b_ref, buf_ref, emit)


def _conf_body(nct, seg_c, seg_l, val_ref, gt_ref, w_ref, b_ref, lg_ref, lb_ref, o_ref, buf_ref, acc_ref):
    i = pl.program_id(0)

    @pl.when(i < nct)
    def _():
        _conf_conv(seg_c, val_ref, gt_ref, w_ref, b_ref, buf_ref, acc_ref)

    @pl.when(i >= nct)
    def _():
        _conf_conv(seg_l, val_ref, gt_ref, w_ref, b_ref, buf_ref, acc_ref)

    acc = acc_ref[...]
    mu = jnp.mean(acc, axis=-1, keepdims=True)
    cen = acc - mu
    var = jnp.mean(cen * cen, axis=-1, keepdims=True)
    y = cen * lax.rsqrt(var + EPS) * lg_ref[...] + lb_ref[...]
    o_ref[...] = _silu(y).astype(BF16)


def _conformer(proj, col0, w, b, lg, lb, n_ctx_rows, ctx_len):
    m = proj.shape[0]
    k, ch = w.shape
    tm = ctx_len
    cb0 = col0 // ch
    vec = lambda v: v.reshape(1, ch)
    vspec = pl.BlockSpec((1, ch), lambda i: (0, 0))
    return pl.pallas_call(
        functools.partial(_conf_body, n_ctx_rows // tm, ctx_len, GRID_W),
        grid=(m // tm,),
        in_specs=[pl.BlockSpec((tm, ch), lambda i: (i, cb0)),
                  pl.BlockSpec((tm, ch), lambda i: (i, cb0 + 1)),
                  pl.BlockSpec((k, ch), lambda i: (0, 0)),
                  vspec, vspec, vspec],
        out_specs=pl.BlockSpec((tm, ch), lambda i: (i, 0)),
        out_shape=jax.ShapeDtypeStruct((m, ch), BF16),
        scratch_shapes=[pltpu.VMEM((SUBLANES, tm // GRID_W * (GRID_W + CONV_PAD) + CONV_PAD, ch), F32),
                        pltpu.VMEM((tm, ch), F32)],
        compiler_params=_cparams(1, 40),
        name="conformer",
    )(proj, proj, w, vec(b), vec(lg), vec(lb))


def _sconv_body(nct, seg_c, bg_ref, cg_ref, hx_ref, w_ref, o_ref):
    i = pl.program_id(0)
    v = cg_ref[...] * hx_ref[...]
    n = v.shape[0]
    it = lax.broadcasted_iota(I32, v.shape, 0)
    w0, w1, w2 = w_ref[0:1, :], w_ref[1:2, :], w_ref[2:3, :]

    @pl.when(i < nct)
    def _():
        pos = it & (seg_c - 1)
        y = w1 * v + w0 * _shift_rows(v, -1, pos, seg_c) + w2 * _shift_rows(v, 1, pos, seg_c)
        o_ref[...] = (bg_ref[...] * y).astype(BF16)

    @pl.when(i >= nct)
    def _():
        y = w1 * v + w0 * _shift_rows(v, -GRID_W, it, n) + w2 * _shift_rows(v, GRID_W, it, n)
        o_ref[...] = (bg_ref[...] * y).astype(BF16)


def _sconv(proj, col0, w, n_ctx_rows, ctx_len, lat_len):
    m = proj.shape[0]
    k, ch = w.shape
    tr, tc = lat_len, 256
    cb0 = col0 // tc
    nch = ch // tc
    return pl.pallas_call(
        functools.partial(_sconv_body, n_ctx_rows // tr, ctx_len),
        grid=(m // tr, nch),
        in_specs=[pl.BlockSpec((tr, tc), lambda i, j: (i, cb0 + j)),
                  pl.BlockSpec((tr, tc), lambda i, j: (i, cb0 + nch + j)),
                  pl.BlockSpec((tr, tc), lambda i, j: (i, cb0 + 2 * nch + j)),
                  pl.BlockSpec((k, tc), lambda i, j: (0, j))],
        out_specs=pl.BlockSpec((tr, tc), lambda i, j: (i, j)),
        out_shape=jax.ShapeDtypeStruct((m, ch), BF16),
        compiler_params=_cparams(2, 40),
        name="sconv",
    )(proj, proj, proj, w)


def _outproj_body(y_ref, u_ref, v_ref, w_ref, x_ref, g_ref, o_ref):
    ky, ku = y_ref.shape[1], u_ref.shape[1]
    acc = jnp.dot(y_ref[...], w_ref[0:ky, :], preferred_element_type=F32)
    acc = acc + jnp.dot(u_ref[...], w_ref[ky:ky + ku, :], preferred_element_type=F32)
    acc = acc + jnp.dot(v_ref[...], w_ref[ky + ku:, :], preferred_element_type=F32)
    o_ref[...] = x_ref[...] + g_ref[0] * acc


def _outproj(y, u, v, w, x, mod, n_ctx_rows, lat_len):
    m, d = x.shape
    kt = w.shape[0]
    tm, tn = 1024, 512
    mrow = functools.partial(_mod_row, tm=tm, n_ctx_rows=n_ctx_rows, lat_len=lat_len)
    return pl.pallas_call(
        _outproj_body,
        grid=(m // tm, d // tn),
        in_specs=[pl.BlockSpec((tm, y.shape[1]), lambda i, j: (i, 0)),
                  pl.BlockSpec((tm, u.shape[1]), lambda i, j: (i, 0)),
                  pl.BlockSpec((tm, v.shape[1]), lambda i, j: (i, 0)),
                  pl.BlockSpec((kt, tn), lambda i, j: (0, j)),
                  pl.BlockSpec((tm, tn), lambda i, j: (i, j)),
                  pl.BlockSpec((1, 1, tn), lambda i, j: (mrow(i) * N_MOD + 2, 0, j))],
        out_specs=pl.BlockSpec((tm, tn), lambda i, j: (i, j)),
        out_shape=jax.ShapeDtypeStruct((m, d), F32),
        compiler_params=_cparams(2, 48),
        name="outproj",
    )(y, u, v, w, x, mod)


def _router_body(n_exp, x_ref, g_ref, sc_ref, sh_ref, rw_ref, hn_ref, aff_t_ref):
    hn = _modnorm(x_ref[...], g_ref[...], sc_ref[0], sh_ref[0])
    hn_hi = hn.astype(BF16)
    hn_ref[...] = hn_hi
    hn_lo = (hn - hn_hi.astype(F32)).astype(BF16)
    w_hi = rw_ref[...].astype(BF16)
    w_lo = (rw_ref[...] - w_hi.astype(F32)).astype(BF16)
    logits = (jnp.dot(hn_hi, w_hi, preferred_element_type=F32) + jnp.dot(hn_hi, w_lo, preferred_element_type=F32)
              + jnp.dot(hn_lo, w_hi, preferred_element_type=F32))
    lane = lax.broadcasted_iota(I32, logits.shape, 1)
    logits = jnp.where(lane < n_exp, logits, NEG)
    e = jnp.exp(logits - jnp.max(logits, axis=-1, keepdims=True))
    aff = e / jnp.sum(e, axis=-1, keepdims=True)
    aff_t_ref[...] = aff.T[:n_exp, :]


def _router(x, g, mod, rw_pad, n_exp, n_ctx_rows, lat_len):
    m, d = x.shape
    tm = 256
    mrow = functools.partial(_mod_row, tm=tm, n_ctx_rows=n_ctx_rows, lat_len=lat_len)
    return pl.pallas_call(
        functools.partial(_router_body, n_exp),
        grid=(m // tm,),
        in_specs=[pl.BlockSpec((tm, d), lambda i: (i, 0)),
                  pl.BlockSpec((1, d), lambda i: (0, 0)),
                  pl.BlockSpec((1, 1, d), lambda i: (mrow(i) * N_MOD + 4, 0, 0)),
                  pl.BlockSpec((1, 1, d), lambda i: (mrow(i) * N_MOD + 3, 0, 0)),
                  pl.BlockSpec((d, LANES), lambda i: (0, 0))],
        out_specs=[pl.BlockSpec((tm, d), lambda i: (i, 0)),
                   pl.BlockSpec((n_exp, tm), lambda i: (0, i))],
        out_shape=[jax.ShapeDtypeStruct((m, d), BF16), jax.ShapeDtypeStruct((n_exp, m), F32)],
        compiler_params=_cparams(1, 40),
        name="router",
    )(x, g.reshape(1, d), mod, mod, rw_pad)


def _prefix_excl(mask):
    rows, t = mask.shape
    blk = min(t, 256)
    mb = jnp.where(mask, 1.0, 0.0)
    upper = jnp.where(lax.broadcasted_iota(I32, (blk, blk), 0) < lax.broadcasted_iota(I32, (blk, blk), 1), 1.0, 0.0)
    upper = upper.astype(BF16)
    carry = jnp.zeros((rows, 1), F32)
    outs = []
    for j in range(t // blk):
        part = mb[:, j * blk:(j + 1) * blk]
        outs.append(jnp.dot(part.astype(BF16), upper, preferred_element_type=F32) + carry)
        carry = carry + jnp.sum(part, axis=1, keepdims=True)
    return outs[0] if len(outs) == 1 else jnp.concatenate(outs, axis=1)


ROUTE_ROWS = 1024
ROUTE_COLS = 1024


def _route_body(cap, n_group, *refs):
    aff_ref, hn_ref = refs[:2]
    xe_ref, gs_ref, pos_t_ref, pos_ref = refs[-4:]
    e = pl.program_id(1)
    n_exp, t = aff_ref.shape

    @pl.when(e == 0)
    def _():
        a = aff_ref[...]
        bits = lax.bitcast_convert_type(a, I32)
        thr = jnp.zeros((n_exp, 1), I32)
        count_ge = lambda cand: jnp.sum(jnp.where(bits >= cand, 1.0, 0.0), axis=1, keepdims=True)
        for bit in range(29, 0, -2):
            c1, c2, c3 = thr | (1 << bit), thr | (2 << bit), thr | (3 << bit)
            n1, n2, n3 = count_ge(c1), count_ge(c2), count_ge(c3)
            thr = jnp.where(n3 >= cap, c3, jnp.where(n2 >= cap, c2, jnp.where(n1 >= cap, c1, thr)))
        c1 = thr | 1
        thr = jnp.where(count_ge(c1) >= cap, c1, thr)
        gt = bits > thr
        eq = bits == thr
        need = cap - jnp.sum(jnp.where(gt, 1.0, 0.0), axis=1, keepdims=True)
        sel = gt | (eq & (_prefix_excl(eq) < need))
        pos = jnp.where(sel, _prefix_excl(sel), -1.0)
        pos_ref[...] = pos
        pad = jnp.full((LANES - n_exp, t), -1.0, F32)
        pos_t_ref[...] = jnp.concatenate([pos, pad], axis=0).T

    slot = lax.broadcasted_iota(I32, (cap, t), 0).astype(F32)
    pieces = []
    for k in range(n_group):
        ek = e * n_group + k
        onehot = pos_ref[pl.ds(ek, 1), :] == slot
        pieces.append(jnp.where(onehot, 1.0, 0.0).astype(BF16))
        gs = jnp.sum(jnp.where(onehot, aff_ref[pl.ds(ek, 1), :], 0.0), axis=1, keepdims=True)
        gs_ref[k] = jnp.broadcast_to(gs, (cap, LANES))
    sel = jnp.concatenate(pieces, axis=0)
    d = hn_ref.shape[1]
    for n0 in range(0, d, ROUTE_COLS):
        got = jnp.dot(sel, hn_ref[:, n0:n0 + ROUTE_COLS], preferred_element_type=F32).astype(BF16)
        for k in range(n_group):
            xe_ref[k, :, n0:n0 + ROUTE_COLS] = got[k * cap:(k + 1) * cap, :]


def _route_gather(aff_t, hn, prev, n_exp, n_seq, t, row0, slot0, slots_total):
    m, d = hn.shape
    cap = EC_CAPACITY_FACTOR * t // n_exp
    rb0, sb0 = row0 // t, slot0 // cap
    in_specs = [pl.BlockSpec((n_exp, t), lambda b, e: (0, rb0 + b)),
                pl.BlockSpec((t, d), lambda b, e: (rb0 + b, 0))]
    args = [aff_t, hn]
    aliases = {}
    if prev is not None:
        in_specs += [pl.BlockSpec(memory_space=pl.ANY), pl.BlockSpec(memory_space=pl.ANY)]
        args += list(prev)
        aliases = {2: 0, 3: 1}
    eg = min(n_exp, ROUTE_ROWS // cap)
    return pl.pallas_call(
        functools.partial(_route_body, cap, eg),
        grid=(n_seq, n_exp // eg),
        in_specs=in_specs,
        out_specs=[pl.BlockSpec((eg, cap, d), lambda b, e: (e, sb0 + b, 0)),
                   pl.BlockSpec((eg, cap, LANES), lambda b, e: (e, sb0 + b, 0)),
                   pl.BlockSpec((t, LANES), lambda b, e: (b, 0))],
        out_shape=[jax.ShapeDtypeStruct((n_exp, slots_total, d), BF16),
                   jax.ShapeDtypeStruct((n_exp, slots_total, LANES), F32),
                   jax.ShapeDtypeStruct((n_seq * t, LANES), F32)],
        scratch_shapes=[pltpu.VMEM((n_exp, t), F32)],
        input_output_aliases=aliases,
        compiler_params=_cparams(2, 48),
        name="route_gather",
    )(*args)


FFN_ROWS = 512


def _ffn_up_body(xe_ref, wg_ref, wu_ref, o_ref, wgb_ref, wub_ref):
    wgb_ref[...] = wg_ref[0, 0].astype(BF16)
    wub_ref[...] = wu_ref[0, 0].astype(BF16)

    def rows(r, carry):
        sl = pl.ds(pl.multiple_of(r * FFN_ROWS, FFN_ROWS), FFN_ROWS)
        x = xe_ref[0, sl, :]
        gte = jnp.dot(x, wgb_ref[...], preferred_element_type=F32)
        up = jnp.dot(x, wub_ref[...], preferred_element_type=F32)
        o_ref[0, sl, :] = (_silu(gte) * up).astype(BF16)
        return carry

    lax.fori_loop(0, xe_ref.shape[1] // FFN_ROWS, rows, 0, unroll=True)


def _ffn_down_body(h_ref, wd_ref, gs_ref, o_ref, wdb_ref):
    wdb_ref[...] = wd_ref[0, 0].astype(BF16)

    def rows(r, carry):
        sl = pl.ds(pl.multiple_of(r * FFN_ROWS, FFN_ROWS), FFN_ROWS)
        y = jnp.dot(h_ref[0, sl, :], wdb_ref[...], preferred_element_type=F32)
        o_ref[0, sl, :] = (y * gs_ref[0, sl, 0:1]).astype(BF16)
        return carry

    lax.fori_loop(0, h_ref.shape[1] // FFN_ROWS, rows, 0, unroll=True)


def _expert_ffn(xe, gs, w_gate, w_up, w_down, layer):
    n_exp, slots, d = xe.shape
    ff = w_gate.shape[3]
    tf, tn = 256, 512
    hid = pl.pallas_call(
        _ffn_up_body,
        grid=(n_exp, ff // tf),
        in_specs=[pl.BlockSpec((1, slots, d), lambda e, f: (e, 0, 0)),
                  pl.BlockSpec((1, 1, d, tf), lambda e, f: (layer, e, 0, f)),
                  pl.BlockSpec((1, 1, d, tf), lambda e, f: (layer, e, 0, f))],
        out_specs=pl.BlockSpec((1, slots, tf), lambda e, f: (e, 0, f)),
        out_shape=jax.ShapeDtypeStruct((n_exp, slots, ff), BF16),
        scratch_shapes=[pltpu.VMEM((d, tf), BF16), pltpu.VMEM((d, tf), BF16)],
        compiler_params=_cparams(2, 48),
        name="ffn_up",
    )(xe, w_gate, w_up)
    return pl.pallas_call(
        _ffn_down_body,
        grid=(n_exp, d // tn),
        in_specs=[pl.BlockSpec((1, slots, ff), lambda e, j: (e, 0, 0)),
                  pl.BlockSpec((1, 1, ff, tn), lambda e, j: (layer, e, 0, j)),
                  pl.BlockSpec((1, slots, LANES), lambda e, j: (e, 0, 0))],
        out_specs=pl.BlockSpec((1, slots, tn), lambda e, j: (e, 0, j)),
        out_shape=jax.ShapeDtypeStruct((n_exp, slots, d), BF16),
        scratch_shapes=[pltpu.VMEM((ff, tn), BF16)],
        compiler_params=_cparams(2, 48),
        name="ffn_down",
    )(hid, w_down, gs)


COMBINE_COLS = 512
COMBINE_BLOCK_ELEMS = 512 * 1024


def _combine_body(cap, pos_t_ref, ye_ref, x_ref, g_ref, o_ref, onehot_ref):
    n_exp = ye_ref.shape[0]
    t, width = onehot_ref.shape

    @pl.when(pl.program_id(1) == 0)
    def _():
        pos_b = pos_t_ref[...].astype(BF16)
        cols = min(COMBINE_COLS, width)
        for c0 in range(0, width, cols):
            col = c0 + lax.broadcasted_iota(I32, (LANES, cols), 1)
            expand = jnp.where(col // cap == lax.broadcasted_iota(I32, (LANES, cols), 0), 1.0, 0.0)
            spread = jnp.dot(pos_b, expand.astype(BF16), preferred_element_type=F32)
            want = ((c0 + lax.broadcasted_iota(I32, (1, cols), 1)) & (cap - 1)).astype(F32)
            onehot_ref[:, c0:c0 + cols] = jnp.where(spread == want, 1.0, 0.0).astype(BF16)

    acc = jnp.dot(onehot_ref[...], ye_ref[...].reshape(n_exp * cap, ye_ref.shape[2]), preferred_element_type=F32)
    o_ref[...] = x_ref[...] + g_ref[0] * acc


def _combine(pos_t, ye, x, mod, n_seq, t, row0, slot0, mod_row0, mod_step):
    m, d = x.shape
    n_exp = ye.shape[0]
    cap = EC_CAPACITY_FACTOR * t // n_exp
    tn = max(256, min(d, COMBINE_BLOCK_ELEMS // t))
    rb0, sb0 = row0 // t, slot0 // cap
    in_specs = [pl.BlockSpec((t, LANES), lambda b, j: (b, 0)),
                pl.BlockSpec((n_exp, cap, tn), lambda b, j: (0, sb0 + b, j)),
                pl.BlockSpec((t, tn), lambda b, j: (rb0 + b, j)),
                pl.BlockSpec((1, 1, tn), lambda b, j: ((mod_row0 + mod_step * b) * N_MOD + 5, 0, j))]
    args = [pos_t, ye, x, mod]
    aliases = {2: 0}
    return pl.pallas_call(
        functools.partial(_combine_body, cap),
        grid=(n_seq, d // tn),
        in_specs=in_specs,
        out_specs=pl.BlockSpec((t, tn), lambda b, j: (rb0 + b, j)),
        out_shape=jax.ShapeDtypeStruct((m, d), F32),
        scratch_shapes=[pltpu.VMEM((t, n_exp * cap), BF16)],
        input_output_aliases=aliases,
        compiler_params=_cparams(2, 48),
        name="combine",
    )(*args)


def _final_body(x_ref, g_ref, o_ref):
    x = x_ref[...]
    o_ref[...] = x * lax.rsqrt(jnp.mean(x * x, axis=-1, keepdims=True) + EPS) * g_ref[...]


def _final_norm(x, g, row0, rows):
    d = x.shape[1]
    tm = 512
    return pl.pallas_call(
        _final_body,
        grid=(rows // tm,),
        in_specs=[pl.BlockSpec((tm, d), lambda i: (row0 // tm + i, 0)),
                  pl.BlockSpec((1, d), lambda i: (0, 0))],
        out_specs=pl.BlockSpec((tm, d), lambda i: (i, 0)),
        out_shape=jax.ShapeDtypeStruct((rows, d), F32),
        compiler_params=_cparams(1, 40),
        name="final_norm",
    )(x, g.reshape(1, d))


def kernel(x_prompt, x_sample, state_ssd_fwd, state_ssd_bwd, c, c_ctx, w_ada, b_ada, norm_mix_g, norm_ffn_g, w_in, ssd_conv_w, ssd_conv_b, ssd_dt_bias, ssd_a_log, ssd_d, ssd_norm_g, conf_conv_w, conf_conv_b, conf_ln_g, conf_ln_b, sc_conv_w, w_out, router_w, w_gate, w_up, w_down, norm_final_g):
    nb, ctx_len, d = x_prompt.shape
    nd, lat_len, _ = x_sample.shape
    depth = w_in.shape[0]
    n_ctx_rows, n_lat_rows = nb * ctx_len, nd * lat_len
    heads = ssd_d.shape[1]
    hp = heads * SSD_HEAD_DIM
    gn = SSD_GROUPS * SSD_D_STATE
    conv_ch = hp + 2 * gn
    conf_w = conf_conv_w.shape[2]
    sc_w = sc_conv_w.shape[2]
    n_exp = router_w.shape[2]
    off_xbc = hp
    off_dt = off_xbc + conv_ch
    off_conf = off_dt + 2 * heads
    off_sc = off_conf + 2 * conf_w
    p_xbc, p_conf, p_sc = hp, hp + conv_ch, hp + conv_ch + 2 * conf_w

    x = jnp.concatenate([x_prompt.reshape(n_ctx_rows, d), x_sample.reshape(n_lat_rows, d)], axis=0)
    cond_rows = 16
    cond = jnp.concatenate([c_ctx[None, :], c, jnp.zeros((cond_rows - 1 - nd, d), F32)], axis=0)
    mod_all = _adaln_all(cond, w_ada, b_ada)

    st_f = state_ssd_fwd.reshape(nd, depth, hp, SSD_D_STATE)
    st_b = state_ssd_bwd.reshape(nd, depth, hp, SSD_D_STATE)
    ctx_cap = EC_CAPACITY_FACTOR * ctx_len // n_exp
    lat_cap = EC_CAPACITY_FACTOR * lat_len // n_exp
    slots_ctx = nb * ctx_cap
    slots_total = slots_ctx + nd * lat_cap
    zpad = jnp.zeros((d, LANES - heads), F32)
    lane_pad = lambda v: jnp.pad(v, ((0, 0), (0, LANES - heads))).reshape(2, 1, LANES)

    ymix = jnp.zeros((n_ctx_rows + n_lat_rows, hp), BF16)
    xe = jnp.zeros((n_exp, slots_total, d), BF16)
    gs = jnp.zeros((n_exp, slots_total, LANES), F32)

    new_f, new_b = [], []
    for l in range(depth):
        mod = mod_all[l].reshape(cond_rows * N_MOD, 1, d)
        wl = w_in[l]
        w_main = jnp.concatenate([wl[:, :off_dt], wl[:, off_conf:]], axis=1).astype(BF16)
        w_dt = jnp.concatenate([wl[:, off_dt:off_dt + heads], zpad, wl[:, off_dt + heads:off_conf], zpad],
                               axis=1).astype(BF16)
        proj, dtraw = _inproj(x, norm_mix_g[l], mod, w_main, w_dt, n_ctx_rows, lat_len)

        xbc = _ssdconv(proj, p_xbc, ssd_conv_w[l], ssd_conv_b[l], n_ctx_rows, ctx_len, lat_len)
        bias, alog = lane_pad(ssd_dt_bias[l]), lane_pad(ssd_a_log[l])
        yf_c, fin_f = _ssd_scan(False, xbc, dtraw, bias, alog, None, None, None, l, nb, ctx_len, 0, True)
        (yf_l,) = _ssd_scan(False, xbc, dtraw, bias, alog, None, None, st_f, l, nd, lat_len, n_ctx_rows, False)
        dexp = jnp.repeat(ssd_d[l], SSD_HEAD_DIM)
        ymix, fin_b = _ssd_scan(True, xbc, dtraw, bias, alog, (yf_c, proj, dexp, ssd_norm_g[l]), ymix, None, l,
                                nb, ctx_len, 0, True)
        (ymix,) = _ssd_scan(True, xbc, dtraw, bias, alog, (yf_l, proj, dexp, ssd_norm_g[l]), ymix, st_b, l,
                            nd, lat_len, n_ctx_rows, False)
        new_f.append(fin_f.reshape(nb, heads, SSD_HEAD_DIM, SSD_D_STATE))
        new_b.append(fin_b.reshape(nb, heads, SSD_HEAD_DIM, SSD_D_STATE))
        u = _conformer(proj, p_conf, conf_conv_w[l], conf_conv_b[l], conf_ln_g[l], conf_ln_b[l], n_ctx_rows, ctx_len)
        v = _sconv(proj, p_sc, sc_conv_w[l], n_ctx_rows, ctx_len, lat_len)
        x = _outproj(ymix, u, v, w_out[l].astype(BF16), x, mod, n_ctx_rows, lat_len)

        rw_pad = jnp.pad(router_w[l], ((0, 0), (0, LANES - n_exp)))
        hn, aff_t = _router(x, norm_ffn_g[l], mod, rw_pad, n_exp, n_ctx_rows, lat_len)
        xe, gs, pos_c = _route_gather(aff_t, hn, (xe, gs), n_exp, nb, ctx_len, 0, 0, slots_total)
        xe, gs, pos_l = _route_gather(aff_t, hn, (xe, gs), n_exp, nd, lat_len, n_ctx_rows, slots_ctx, slots_total)
        ye = _expert_ffn(xe, gs, w_gate, w_up, w_down, l)
        x = _combine(pos_c, ye, x, mod, nb, ctx_len, 0, 0, 0, 0)
        x = _combine(pos_l, ye, x, mod, nd, lat_len, n_ctx_rows, slots_ctx, 1, 1)

    y_prompt = _final_norm(x, norm_final_g, 0, n_ctx_rows).reshape(nb, ctx_len, d)
    y_sample = _final_norm(x, norm_final_g, n_ctx_rows, n_lat_rows).reshape(nd, lat_len, d)
    return (y_prompt, y_sample, jnp.stack(new_f, axis=1), jnp.stack(new_b, axis=1))
```
